```python
import math
import jax, jax.numpy as jnp
from jax import lax
import numpy as np


D_MODEL = 1024
BATCH = 4
SEQ = 8192
DEPTH = 4
DEC_BATCH = 16
DEC_SEQ = 32
PAST_LEN = 2048

CHUNK = 64
N_ATT = 8
ATT_HD = 64
ATT_W = N_ATT * 2 * ATT_HD
N_HG = 8
HG_DK = 128
HG_DV = 128
HG_KW = N_HG * HG_DK
HG_VW = N_HG * HG_DV
D_FF = 2816
N_ADA = 9
Q_BLOCK = 128
MACARON_WEIGHT = 0.5
EPS = 1e-6
MASK_VALUE = -1e30
TINY = 1e-30
IN_SPLIT_POINTS = (ATT_W, 2 * ATT_W, 3 * ATT_W,
                   3 * ATT_W + HG_KW, 3 * ATT_W + 2 * HG_KW,
                   3 * ATT_W + 2 * HG_KW + HG_VW, 3 * ATT_W + 2 * HG_KW + 2 * HG_VW,
                   3 * ATT_W + 2 * HG_KW + 2 * HG_VW + D_MODEL)
D_IN = 3 * ATT_W + 2 * HG_KW + 2 * HG_VW + 2 * D_MODEL

kernel_name = 'hybrid_diffattn_hgrn2_macaron_adaln_stream_step'


def rmsnorm(x, g):
    xf = x.astype(jnp.float32)
    y = xf * lax.rsqrt(jnp.mean(xf * xf, axis=-1, keepdims=True) + EPS)
    return (y * g.astype(jnp.float32)).astype(x.dtype)


def swiglu(h, w_gu, w_d):
    gate, up = jnp.split(h @ w_gu, 2, axis=-1)
    return (jax.nn.silu(gate) * up) @ w_d


def diff_attention(q, k, v, q_pos, k_pos, lam):
    B, Tq = q.shape[:2]
    kf = k.astype(jnp.float32)
    vf = v.astype(jnp.float32)

    def block(args):
        qb, pb = args
        s = jnp.einsum('bqhmd,bkhmd->bhmqk', qb.astype(jnp.float32), kf) * (ATT_HD ** -0.5)
        visible = k_pos[None, :] < ((pb // CHUNK + 1) * CHUNK)[:, None]
        p = jax.nn.softmax(jnp.where(visible, s, MASK_VALUE), axis=-1)
        a = p[:, :, 0] - lam * p[:, :, 1]
        return jnp.einsum('bhqk,bkhe->bqhe', a, vf)

    if Tq > Q_BLOCK:
        nb = Tq // Q_BLOCK
        qb = jnp.moveaxis(q.reshape(B, nb, Q_BLOCK, *q.shape[2:]), 1, 0)
        pb = q_pos.reshape(nb, Q_BLOCK)
        o = lax.map(block, (qb, pb))
        return jnp.moveaxis(o, 0, 1).reshape(B, Tq, *o.shape[3:])
    return block((q, q_pos))


def hgrn2_chunked(q, k, v, logf, s0):
    B, T = q.shape[:2]
    L = min(CHUNK, T)
    n = T // L

    def to_chunks(a):
        return jnp.moveaxis(a.reshape(B, n, L, *a.shape[2:]), 1, 0)

    causal = jnp.tril(jnp.ones((L, L), dtype=bool))[None, :, :, None, None]

    def step(S, inp):
        qc, kc, vc, lfc = inp
        b = lax.cumsum(lfc, axis=1)
        b_last = b[:, -1]
        o_inter = jnp.einsum('bthk,bhkv->bthv', qc * jnp.exp(b), S)
        rel = b[:, :, None] - b[:, None, :]
        decay = jnp.where(causal, jnp.exp(jnp.where(causal, rel, 0.0)), 0.0)
        A = jnp.einsum('bthk,btshk,bshk->bhts', qc, decay, kc)
        o_intra = jnp.einsum('bhts,bshv->bthv', A, vc)
        S_new = S * jnp.exp(b_last)[..., None] + jnp.einsum(
            'bshk,bshv->bhkv', kc * jnp.exp(b_last[:, None] - b), vc)
        return S_new, o_inter + o_intra

    s_fin, o = lax.scan(step, s0, (to_chunks(q), to_chunks(k), to_chunks(v), to_chunks(logf)))
    o = jnp.moveaxis(o, 0, 1).reshape(B, T, *o.shape[3:])
    return o, s_fin


def trunk(x, c, q_pos, k_pos, past_k, past_v, past_s, p):
    B, T, _ = x.shape
    f32 = jnp.float32
    lb_sm = jax.nn.softmax(p['hg_lb_logits'].astype(f32), axis=0)
    lb_all = lax.cumsum(lb_sm, axis=0) - lb_sm[0]
    c_act = jax.nn.silu(c.astype(f32))
    new_k, new_v, new_s = [], [], []
    for l in range(DEPTH):
        m = (c_act @ p['w_ada'][l].astype(f32) + p['b_ada'][l].astype(f32)).astype(x.dtype)
        m = m.reshape(B, N_ADA, 1, D_MODEL)
        sh1, sc1, g1, sh2, sc2, g2, sh3, sc3, g3 = [m[:, i] for i in range(N_ADA)]

        h = rmsnorm(x, p['g_ffn1'][l]) * (1 + sc1) + sh1
        x = x + MACARON_WEIGHT * g1 * swiglu(h, p['w_ffn1_gu'][l], p['w_ffn1_d'][l])

        u = rmsnorm(x, p['g_mix'][l]) * (1 + sc2) + sh2
        z = u @ p['w_in'][l]
        zq, zk, zv, hq, hf, hi, hg, za, zh = jnp.split(z, IN_SPLIT_POINTS, axis=-1)

        q = zq.reshape(B, T, N_ATT, 2, ATT_HD)
        k_rows = zk.reshape(B, T, N_ATT, 2 * ATT_HD)
        v_rows = zv.reshape(B, T, N_ATT, 2 * ATT_HD)
        new_k.append(k_rows)
        new_v.append(v_rows)
        if past_k is None:
            k_all, v_all = k_rows, v_rows
        else:
            k_all = jnp.concatenate([past_k[l].astype(k_rows.dtype), k_rows], axis=1)
            v_all = jnp.concatenate([past_v[l].astype(v_rows.dtype), v_rows], axis=1)
        lam_init = 0.8 - 0.6 * math.exp(-0.3 * l)
        lp = p['att_lambda'][l].astype(f32)
        lam = jnp.exp(jnp.sum(lp[0] * lp[1])) - jnp.exp(jnp.sum(lp[2] * lp[3])) + lam_init
        o_att = diff_attention(q, k_all.reshape(B, -1, N_ATT, 2, ATT_HD), v_all, q_pos, k_pos, lam)
        o_att = rmsnorm(o_att, p['g_att_sub'][l]) * (1.0 - lam_init)
        o_att = o_att.reshape(B, T, ATT_W).astype(x.dtype)

        lb = lb_all[l].reshape(N_HG, HG_DK)
        fr = hf.astype(f32).reshape(B, T, N_HG, HG_DK)
        sig = jax.nn.sigmoid(fr)
        f_gate = lb + (1.0 - lb) * sig
        logf = jnp.log(jnp.maximum(f_gate, TINY))
        kk = (1.0 - lb) * (1.0 - sig)
        qq = jax.nn.silu(hq.astype(f32)).reshape(B, T, N_HG, HG_DK)
        vv = hi.astype(f32).reshape(B, T, N_HG, HG_DV)
        s0 = jnp.zeros((B, N_HG, HG_DK, HG_DV), f32) if past_s is None else past_s[l].astype(f32)
        o_h, s_fin = hgrn2_chunked(qq, kk, vv, logf, s0)
        new_s.append(s_fin.astype(x.dtype))
        o_h = rmsnorm(o_h, p['g_hg_norm'][l]) * jax.nn.silu(hg.astype(f32).reshape(B, T, N_HG, HG_DV))
        o_h = o_h.reshape(B, T, HG_VW).astype(x.dtype)

        merged = (jax.nn.sigmoid(za) * (o_att @ p['w_br_att'][l])
                  + jax.nn.sigmoid(zh) * (o_h @ p['w_br_hg'][l]))
        x = x + g2 * (merged @ p['w_out'][l])

        h = rmsnorm(x, p['g_ffn2'][l]) * (1 + sc3) + sh3
        x = x + MACARON_WEIGHT * g3 * swiglu(h, p['w_ffn2_gu'][l], p['w_ffn2_d'][l])
    y = rmsnorm(x, p['g_final'])
    return y, jnp.stack(new_k), jnp.stack(new_v), jnp.stack(new_s)


def setup_inputs(seed: int = 0) -> dict:
    key = jax.random.key(seed)
    ks = jax.random.split(key, 32)
    f32 = jnp.float32

    def nrm(k, shape, scale):
        return jax.random.normal(k, shape, f32) * scale

    def gain(k, shape):
        return 1.0 + 0.02 * jax.random.normal(k, shape, f32)

    return {
        'x_prompt': nrm(ks[0], (BATCH, SEQ, D_MODEL), 1.0),
        'x_sample': nrm(ks[1], (DEC_BATCH, DEC_SEQ, D_MODEL), 1.0),
        'cache_k': nrm(ks[2], (DEPTH, DEC_BATCH, PAST_LEN, N_ATT, 2 * ATT_HD), 1.0),
        'cache_v': nrm(ks[3], (DEPTH, DEC_BATCH, PAST_LEN, N_ATT, 2 * ATT_HD), 1.0),
        'state_hgrn': nrm(ks[4], (DEPTH, DEC_BATCH, N_HG, HG_DK, HG_DV), 0.5),
        'c_prompt': nrm(ks[5], (BATCH, D_MODEL), 1.0),
        'c_sample': nrm(ks[6], (DEC_BATCH, D_MODEL), 1.0),
        'w_ada': nrm(ks[7], (DEPTH, D_MODEL, N_ADA * D_MODEL), 0.5 * D_MODEL ** -0.5),
        'b_ada': nrm(ks[8], (DEPTH, N_ADA * D_MODEL), 0.01),
        'g_ffn1': gain(ks[9], (DEPTH, D_MODEL)),
        'w_ffn1_gu': nrm(ks[10], (DEPTH, D_MODEL, 2 * D_FF), D_MODEL ** -0.5),
        'w_ffn1_d': nrm(ks[11], (DEPTH, D_FF, D_MODEL), D_FF ** -0.5),
        'g_mix': gain(ks[12], (DEPTH, D_MODEL)),
        'w_in': nrm(ks[13], (DEPTH, D_MODEL, D_IN), D_MODEL ** -0.5),
        'att_lambda': nrm(ks[14], (DEPTH, 4, ATT_HD), 0.1),
        'g_att_sub': gain(ks[15], (DEPTH, 2 * ATT_HD)),
        'hg_lb_logits': nrm(ks[16], (DEPTH, HG_KW), 0.5),
        'g_hg_norm': gain(ks[17], (DEPTH, HG_DV)),
        'w_br_att': nrm(ks[18], (DEPTH, ATT_W, D_MODEL), ATT_W ** -0.5),
        'w_br_hg': nrm(ks[19], (DEPTH, HG_VW, D_MODEL), HG_VW ** -0.5),
        'w_out': nrm(ks[20], (DEPTH, D_MODEL, D_MODEL), D_MODEL ** -0.5),
        'g_ffn2': gain(ks[21], (DEPTH, D_MODEL)),
        'w_ffn2_gu': nrm(ks[22], (DEPTH, D_MODEL, 2 * D_FF), D_MODEL ** -0.5),
        'w_ffn2_d': nrm(ks[23], (DEPTH, D_FF, D_MODEL), D_FF ** -0.5),
        'g_final': gain(ks[24], (D_MODEL,)),
    }


def reference(x_prompt, x_sample, cache_k, cache_v, state_hgrn, c_prompt, c_sample,
              w_ada, b_ada, g_ffn1, w_ffn1_gu, w_ffn1_d, g_mix, w_in, att_lambda, g_att_sub,
              hg_lb_logits, g_hg_norm, w_br_att, w_br_hg, w_out, g_ffn2, w_ffn2_gu, w_ffn2_d,
              g_final):
    params = dict(w_ada=w_ada, b_ada=b_ada, g_ffn1=g_ffn1, w_ffn1_gu=w_ffn1_gu, w_ffn1_d=w_ffn1_d,
                  g_mix=g_mix, w_in=w_in, att_lambda=att_lambda, g_att_sub=g_att_sub,
                  hg_lb_logits=hg_lb_logits, g_hg_norm=g_hg_norm, w_br_att=w_br_att,
                  w_br_hg=w_br_hg, w_out=w_out, g_ffn2=g_ffn2, w_ffn2_gu=w_ffn2_gu,
                  w_ffn2_d=w_ffn2_d, g_final=g_final)
    t_p = x_prompt.shape[1]
    pos_p = jnp.arange(t_p, dtype=jnp.int32)
    y_prompt, k_prompt, v_prompt, s_prompt = trunk(
        x_prompt, c_prompt, pos_p, pos_p, None, None, None, params)
    past = cache_k.shape[2]
    t_s = x_sample.shape[1]
    pos_s = past + jnp.arange(t_s, dtype=jnp.int32)
    kpos_s = jnp.arange(past + t_s, dtype=jnp.int32)
    y_sample, k_sample, v_sample, s_sample = trunk(
        x_sample, c_sample, pos_s, kpos_s, cache_k, cache_v, state_hgrn, params)
    return (y_prompt, y_sample, k_prompt, v_prompt, s_prompt, k_sample, v_sample, s_sample)
```

```python
import functools
import math

import jax
import jax.numpy as jnp
from jax import lax
from jax.experimental import pallas as pl
from jax.experimental.pallas import tpu as pltpu

F32 = jnp.float32
BF16 = jnp.bfloat16

CHUNK = 64
N_ADA = 9
MACARON_WEIGHT = 0.5
EPS = 1e-6
MASK_VALUE = -1e30
TINY = 1e-30
LANES = 128
SUB = 16
VMEM_LIMIT = 56 * 1024 * 1024

_NT = (((1,), (1,)), ((), ()))
_TN = (((0,), (0,)), ((), ()))


def _params(*sem):
    return pltpu.CompilerParams(dimension_semantics=sem, vmem_limit_bytes=VMEM_LIMIT)


def _resident(shape):
    nd = len(shape)
    return pl.BlockSpec(shape, lambda *_: (0,) * nd, pipeline_mode=pl.Buffered(1))


def _modnorm(x, g, sc, sh):
    y = x * lax.rsqrt(jnp.mean(x * x, axis=-1, keepdims=True) + EPS) * g
    return y * (1.0 + sc) + sh


def _silu(x):
    return x * jax.nn.sigmoid(x)


def _mod_spec(per_row, tm, d, rows_per_seq):
    if per_row:
        return pl.BlockSpec((tm, d), lambda i, *_: (i, 0))
    tiles = rows_per_seq // tm
    return pl.BlockSpec((None, 1, d), lambda i, *_: (i // tiles, 0, 0))


def _ada_kernel(c_ref, w_ref, b_ref, o_ref):
    c = c_ref[...]
    o_ref[...] = jnp.dot(_silu(c).astype(BF16), w_ref[...].astype(BF16),
                         preferred_element_type=F32) + b_ref[...]


def _ada(c_all, w_ada, b_ada):
    depth, d, _ = w_ada.shape
    ns = c_all.shape[0]
    return pl.pallas_call(
        _ada_kernel,
        grid=(depth, N_ADA),
        in_specs=[pl.BlockSpec((ns, d), lambda l, j: (0, 0)),
                  pl.BlockSpec((None, d, d), lambda l, j: (l, 0, j)),
                  pl.BlockSpec((None, 1, d), lambda l, j: (l, 0, j))],
        out_specs=pl.BlockSpec((None, ns, d), lambda l, j: (l, 0, j)),
        out_shape=jax.ShapeDtypeStruct((depth, ns, N_ADA * d), F32),
        compiler_params=_params("parallel", "parallel"),
        name="ada",
    )(c_all, w_ada, b_ada.reshape(depth, 1, N_ADA * d))


def _ffn_kernel(x_ref, g_ref, sh_ref, sc_ref, gt_ref, wg_ref, wu_ref, wd_ref, o_ref,
                h_scr, acc_scr, *, n_chunks):
    h_scr[...] = _modnorm(x_ref[...], g_ref[...], sc_ref[...], sh_ref[...]).astype(BF16)
    acc_scr[...] = jnp.zeros_like(acc_scr)

    def body(c, carry):
        h = h_scr[...]
        gate = jnp.dot(h, wg_ref[c], preferred_element_type=F32)
        up = jnp.dot(h, wu_ref[c], preferred_element_type=F32)
        a = (_silu(gate) * up).astype(BF16)
        acc_scr[...] += jnp.dot(a, wd_ref[c], preferred_element_type=F32)
        return carry

    lax.fori_loop(0, n_chunks, body, 0)
    o_ref[...] = x_ref[...] + MACARON_WEIGHT * gt_ref[...] * acc_scr[...]


def _ffn(x, g, sh, sc, gt, wg, wu, wd, *, tm, per_row, rows_per_seq):
    t, d = x.shape
    n_chunks = wg.shape[0]
    mod = _mod_spec(per_row, tm, d, rows_per_seq)
    row = pl.BlockSpec((tm, d), lambda i: (i, 0))
    return pl.pallas_call(
        functools.partial(_ffn_kernel, n_chunks=n_chunks),
        grid=(t // tm,),
        in_specs=[row, pl.BlockSpec((1, d), lambda i: (0, 0)), mod, mod, mod,
                  _resident(wg.shape), _resident(wu.shape), _resident(wd.shape)],
        out_specs=row,
        out_shape=jax.ShapeDtypeStruct((t, d), F32),
        scratch_shapes=[pltpu.VMEM((tm, d), BF16), pltpu.VMEM((tm, d), F32)],
        compiler_params=_params("parallel"),
        name="ffn",
    )(x, g, sh, sc, gt, wg, wu, wd)


def _inproj_kernel(x_ref, g_ref, sh_ref, sc_ref, w_ref, kin_ref, vin_ref,
                   z_ref, k_ref, v_ref, u_scr):
    del kin_ref, vin_ref
    j = pl.program_id(1)

    @pl.when(j == 0)
    def _():
        u_scr[...] = _modnorm(x_ref[...], g_ref[...], sc_ref[...], sh_ref[...]).astype(BF16)

    z = jnp.dot(u_scr[...], w_ref[...], preferred_element_type=F32)
    z_ref[...] = z

    @pl.when(j == 1)
    def _():
        k_ref[...] = z

    @pl.when(j == 2)
    def _():
        v_ref[...] = z


def _inproj(x, g, sh, sc, w3, kst, vst, layer, *, tm, per_row, rows_per_seq):
    t, d = x.shape
    ng = w3.shape[0]
    mod = _mod_spec(per_row, tm, d, rows_per_seq)
    row = pl.BlockSpec((tm, d), lambda i, j: (i, 0))
    stack = pl.BlockSpec((None, tm, d), lambda i, j: (layer, i, 0))
    anyspec = pl.BlockSpec(memory_space=pl.ANY)
    return pl.pallas_call(
        _inproj_kernel,
        grid=(t // tm, ng),
        in_specs=[row, pl.BlockSpec((1, d), lambda i, j: (0, 0)), mod, mod,
                  pl.BlockSpec((None, d, d), lambda i, j: (j, 0, 0)), anyspec, anyspec],
        out_specs=[pl.BlockSpec((tm, d), lambda i, j: (i, j)), stack, stack],
        out_shape=[jax.ShapeDtypeStruct((t, ng * d), F32),
                   jax.ShapeDtypeStruct(kst.shape, F32),
                   jax.ShapeDtypeStruct(vst.shape, F32)],
        scratch_shapes=[pltpu.VMEM((tm, d), BF16)],
        input_output_aliases={5: 1, 6: 2},
        compiler_params=_params("parallel", "arbitrary"),
        name="inproj",
    )(x, g, sh, sc, w3, kst, vst)


def _lambda(lamp_ref, lam_init):
    lp = lamp_ref[...]
    s01 = jnp.sum(lp[0:1] * lp[1:2], axis=-1, keepdims=True)
    s23 = jnp.sum(lp[2:3] * lp[3:4], axis=-1, keepdims=True)
    return jnp.exp(s01) - jnp.exp(s23) + lam_init


def _split_maps(q, hd):
    lane = lax.broadcasted_iota(jnp.int32, q.shape, 1)
    return jnp.concatenate([jnp.where(lane < hd, q, 0.0), jnp.where(lane >= hd, q, 0.0)],
                           axis=0).astype(BF16)


def _visible(rows, cols, tq, q0, k0):
    r = lax.broadcasted_iota(jnp.int32, (rows, cols), 0)
    qpos = q0 + jnp.where(r >= tq, r - tq, r)
    kpos = k0 + lax.broadcasted_iota(jnp.int32, (rows, cols), 1)
    return kpos < (qpos // CHUNK + 1) * CHUNK


def _attn_finish(o1, o2, lam, lam_init, gsub):
    o = o1 - lam * o2
    y = o * lax.rsqrt(jnp.mean(o * o, axis=-1, keepdims=True) + EPS) * gsub
    return y * (1.0 - lam_init)


def _attn_kernel(cst_ref, lamp_ref, gsub_ref, q_ref, k_ref, v_ref, o_ref,
                 qs_scr, m_scr, l_scr, acc_scr, *, tq, tk, hd):
    i = pl.program_id(2)
    lam_init = cst_ref[0]
    qs_scr[...] = _split_maps(q_ref[...] * (hd ** -0.5), hd)
    m_scr[...] = jnp.full_like(m_scr, MASK_VALUE)
    l_scr[...] = jnp.zeros_like(l_scr)
    acc_scr[...] = jnp.zeros_like(acc_scr)

    def kv_step(j, masked):
        start = pl.multiple_of(j * tk, tk)
        k = k_ref[pl.ds(start, tk), :].astype(BF16)
        v = v_ref[pl.ds(start, tk), :].astype(BF16)
        s = lax.dot_general(qs_scr[...], k, _NT, preferred_element_type=F32)
        if masked:
            s = jnp.where(_visible(2 * tq, tk, tq, i * tq, j * tk), s, MASK_VALUE)
        m_old = m_scr[...]
        m_new = jnp.maximum(m_old, jnp.max(s, axis=-1, keepdims=True))
        p = jnp.exp(s - m_new)
        alpha = jnp.exp(m_old - m_new)
        l_scr[...] = alpha * l_scr[...] + jnp.sum(p, axis=-1, keepdims=True)
        acc_scr[...] = alpha * acc_scr[...] + jnp.dot(p.astype(BF16), v, preferred_element_type=F32)
        m_scr[...] = m_new

    n_full = i * (tq // tk)

    def full_body(j, carry):
        kv_step(j, False)
        return carry

    lax.fori_loop(0, n_full, full_body, 0)
    for jj in range(tq // tk):
        kv_step(n_full + jj, True)

    o12 = acc_scr[...] / l_scr[...]
    lam = _lambda(lamp_ref, lam_init)
    o_ref[...] = _attn_finish(o12[:tq], o12[tq:], lam, lam_init, gsub_ref[...])


def _attn(cst, lamp, gsub, z, kst, vst, layer, *, nb, t, nh, tq, tk):
    bt, d = z.shape[0], nh * LANES
    nq = t // tq
    kv = pl.BlockSpec((None, t, LANES), lambda b, h, i: (layer, b, h))
    return pl.pallas_call(
        functools.partial(_attn_kernel, tq=tq, tk=tk, hd=LANES // 2),
        grid=(nb, nh, nq),
        in_specs=[pl.BlockSpec(memory_space=pltpu.SMEM),
                  pl.BlockSpec(lamp.shape, lambda b, h, i: (0, 0)),
                  pl.BlockSpec((1, LANES), lambda b, h, i: (0, 0)),
                  pl.BlockSpec((tq, LANES), lambda b, h, i: (b * nq + i, h)),
                  kv, kv],
        out_specs=pl.BlockSpec((tq, LANES), lambda b, h, i: (b * nq + i, h)),
        out_shape=jax.ShapeDtypeStruct((bt, d), F32),
        scratch_shapes=[pltpu.VMEM((2 * tq, LANES), BF16), pltpu.VMEM((2 * tq, 1), F32),
                        pltpu.VMEM((2 * tq, 1), F32), pltpu.VMEM((2 * tq, LANES), F32)],
        compiler_params=_params("parallel", "parallel", "arbitrary"),
        name="attn",
    )(cst, lamp, gsub, z, kst, vst)


def _attn_cached_kernel(cst_ref, lamp_ref, gsub_ref, q_ref, kc_ref, vc_ref, kn_ref, vn_ref, o_ref,
                        *, tq, past, hd):
    lam_init = cst_ref[0]
    qs = _split_maps(q_ref[...] * (hd ** -0.5), hd)

    def scores(k, k0):
        s = lax.dot_general(qs, k.astype(BF16), _NT, preferred_element_type=F32)
        return jnp.where(_visible(2 * tq, k.shape[0], tq, past, k0), s, MASK_VALUE)

    sc = scores(kc_ref[...], 0)
    sn = scores(kn_ref[...], past)
    m = jnp.maximum(jnp.max(sc, axis=-1, keepdims=True), jnp.max(sn, axis=-1, keepdims=True))
    pc = jnp.exp(sc - m)
    pn = jnp.exp(sn - m)
    l = jnp.sum(pc, axis=-1, keepdims=True) + jnp.sum(pn, axis=-1, keepdims=True)
    acc = (jnp.dot(pc.astype(BF16), vc_ref[...].astype(BF16), preferred_element_type=F32)
           + jnp.dot(pn.astype(BF16), vn_ref[...].astype(BF16), preferred_element_type=F32))
    o12 = acc / l
    lam = _lambda(lamp_ref, lam_init)
    o_ref[...] = _attn_finish(o12[:tq], o12[tq:], lam, lam_init, gsub_ref[...])


def _attn_cached(cst, lamp, gsub, z, ck, cv, kst, vst, layer, *, nb, t, nh):
    bt, d = z.shape[0], nh * LANES
    past = ck.shape[2]
    cache = pl.BlockSpec((None, None, past, LANES), lambda b, h: (layer, b, 0, h))
    new = pl.BlockSpec((None, t, LANES), lambda b, h: (layer, b, h))
    return pl.pallas_call(
        functools.partial(_attn_cached_kernel, tq=t, past=past, hd=LANES // 2),
        grid=(nb, nh),
        in_specs=[pl.BlockSpec(memory_space=pltpu.SMEM),
                  pl.BlockSpec(lamp.shape, lambda b, h: (0, 0)),
                  pl.BlockSpec((1, LANES), lambda b, h: (0, 0)),
                  pl.BlockSpec((t, LANES), lambda b, h: (b, h)),
                  cache, cache, new, new],
        out_specs=pl.BlockSpec((t, LANES), lambda b, h: (b, h)),
        out_shape=jax.ShapeDtypeStruct((bt, d), F32),
        compiler_params=_params("parallel", "parallel"),
        name="attn_cached",
    )(cst, lamp, gsub, z, ck, cv, kst, vst)


def _split3(x):
    hi = x.astype(BF16)
    r = x - hi.astype(F32)
    mid = r.astype(BF16)
    lo = (r - mid.astype(F32)).astype(BF16)
    return hi, mid, lo


def _hgrn_kernel(*refs, tt, blk, has_s0):
    if has_s0:
        hq_ref, hf_ref, hi_ref, hg_ref, lb_ref, gn_ref, s0_ref, o_ref, sf_ref, st_scr, pad_scr, o_scr = refs
    else:
        hq_ref, hf_ref, hi_ref, hg_ref, lb_ref, gn_ref, o_ref, sf_ref, st_scr, pad_scr, o_scr = refs
    t = pl.program_id(2)

    @pl.when(t == 0)
    def _():
        if has_s0:
            st_scr[...] = s0_ref[...].T
        else:
            st_scr[...] = jnp.zeros_like(st_scr)
        pad_scr[:, 0:SUB, :] = jnp.zeros((3, SUB, LANES), F32)

    lb = lb_ref[...]
    sig = jax.nn.sigmoid(hf_ref[...])
    logf = jnp.log(jnp.maximum(lb + (1.0 - lb) * sig, TINY))
    kk = (1.0 - lb) * (1.0 - sig)
    qq = _silu(hq_ref[...])
    vv = hi_ref[...]

    row = lax.broadcasted_iota(jnp.int32, (tt, tt), 0)
    col = lax.broadcasted_iota(jnp.int32, (tt, tt), 1)
    tri = jnp.where((col <= row) & (col // blk == row // blk), 1.0, 0.0).astype(BF16)
    b = sum(jnp.dot(tri, part, preferred_element_type=F32) for part in _split3(logf))

    pad_scr[0, SUB:, :] = kk
    pad_scr[1, SUB:, :] = b
    pad_scr[2, SUB:, :] = vv
    pos = lax.broadcasted_iota(jnp.int32, (tt, LANES), 0) % SUB
    ones = jnp.ones((LANES, LANES), BF16)
    o = jnp.zeros((tt, LANES), F32)
    for d in range(SUB):
        ks = pad_scr[0, SUB - d:SUB - d + tt, :]
        bs = pad_scr[1, SUB - d:SUB - d + tt, :]
        vs = pad_scr[2, SUB - d:SUB - d + tt, :]
        db = jnp.where(pos >= d, b - bs, MASK_VALUE)
        p = (qq * ks * jnp.exp(db)).astype(BF16)
        o = o + jnp.dot(p, ones, preferred_element_type=F32) * vs
    o_scr[...] = o

    m = SUB
    while 2 * m <= blk:
        n = tt // (2 * m)
        b4, q4, k4, v4 = (a.reshape(n, 2, m, LANES) for a in (b, qq, kk, vv))
        b_early, b_late = b4[:, 0], b4[:, 1]
        b_edge = b_early[:, m - 1:m, :]
        qt = (q4[:, 1] * jnp.exp(b_late - b_edge)).astype(BF16)
        kt = (k4[:, 0] * jnp.exp(b_edge - b_early)).astype(BF16)
        a = jnp.einsum("nqd,nkd->nqk", qt, kt, preferred_element_type=F32)
        o_late = jnp.einsum("nqk,nkv->nqv", a.astype(BF16), v4[:, 0].astype(BF16),
                            preferred_element_type=F32)
        for g in range(n):
            o_scr[(2 * g + 1) * m:(2 * g + 2) * m, :] += o_late[g]
        m *= 2

    st = st_scr[...]
    for c in range(tt // blk):
        sl = slice(c * blk, (c + 1) * blk)
        bc = b[sl]
        b_last = bc[blk - 1:blk]
        qt = (qq[sl] * jnp.exp(bc)).astype(BF16)
        o_scr[sl, :] += lax.dot_general(qt, st.astype(BF16), _NT, preferred_element_type=F32)
        kt = (kk[sl] * jnp.exp(b_last - bc)).astype(BF16)
        st = st * jnp.exp(b_last) + lax.dot_general(vv[sl].astype(BF16), kt, _TN,
                                                    preferred_element_type=F32)
    st_scr[...] = st

    o = o_scr[...]
    y = o * lax.rsqrt(jnp.mean(o * o, axis=-1, keepdims=True) + EPS) * gn_ref[...]
    o_ref[...] = y * _silu(hg_ref[...])

    @pl.when(t == pl.num_programs(2) - 1)
    def _():
        sf_ref[...] = st.T


def _hgrn(z, lb, gn, s0, layer, *, nb, t, nh, tt, blk):
    bt = z.shape[0]
    nt = t // tt
    has_s0 = s0 is not None

    def col(group):
        return pl.BlockSpec((tt, LANES), lambda b, h, i: (b * nt + i, group * nh + h))

    state = pl.BlockSpec((None, None, LANES, LANES), lambda b, h, i: (b, h, 0, 0))
    in_specs = [col(3), col(4), col(5), col(6),
                pl.BlockSpec((1, LANES), lambda b, h, i: (0, h)),
                pl.BlockSpec((1, LANES), lambda b, h, i: (0, 0))]
    args = [z, z, z, z, lb, gn]
    if has_s0:
        in_specs.append(pl.BlockSpec((None, None, None, LANES, LANES),
                                     lambda b, h, i: (layer, b, h, 0, 0)))
        args.append(s0)
    return pl.pallas_call(
        functools.partial(_hgrn_kernel, tt=tt, blk=blk, has_s0=has_s0),
        grid=(nb, nh, nt),
        in_specs=in_specs,
        out_specs=[pl.BlockSpec((tt, LANES), lambda b, h, i: (b * nt + i, h)), state],
        out_shape=[jax.ShapeDtypeStruct((bt, nh * LANES), F32),
                   jax.ShapeDtypeStruct((nb, nh, LANES, LANES), F32)],
        scratch_shapes=[pltpu.VMEM((LANES, LANES), F32), pltpu.VMEM((3, SUB + tt, LANES), F32),
                        pltpu.VMEM((tt, LANES), F32)],
        compiler_params=_params("parallel", "parallel", "arbitrary"),
        name="hgrn",
    )(*args)


def _merge_kernel(x_ref, gt_ref, oa_ref, oh_ref, za_ref, zh_ref, wa_ref, wh_ref, wo_ref, o_ref):
    a = jnp.dot(oa_ref[...].astype(BF16), wa_ref[...], preferred_element_type=F32)
    r = jnp.dot(oh_ref[...].astype(BF16), wh_ref[...], preferred_element_type=F32)
    merged = jax.nn.sigmoid(za_ref[...]) * a + jax.nn.sigmoid(zh_ref[...]) * r
    o_ref[...] = x_ref[...] + gt_ref[...] * jnp.dot(merged.astype(BF16), wo_ref[...],
                                                    preferred_element_type=F32)


def _merge(x, gt, oa, oh, z, wa, wh, wo, *, tm, per_row, rows_per_seq):
    t, d = x.shape
    mod = _mod_spec(per_row, tm, d, rows_per_seq)
    row = pl.BlockSpec((tm, d), lambda i: (i, 0))
    return pl.pallas_call(
        _merge_kernel,
        grid=(t // tm,),
        in_specs=[row, mod, row, row,
                  pl.BlockSpec((tm, d), lambda i: (i, 7)), pl.BlockSpec((tm, d), lambda i: (i, 8)),
                  _resident(wa.shape), _resident(wh.shape), _resident(wo.shape)],
        out_specs=row,
        out_shape=jax.ShapeDtypeStruct((t, d), F32),
        compiler_params=_params("parallel"),
        name="merge",
    )(x, gt, oa, oh, z, z, wa, wh, wo)


def _norm_kernel(x_ref, g_ref, o_ref):
    x = x_ref[...]
    o_ref[...] = x * lax.rsqrt(jnp.mean(x * x, axis=-1, keepdims=True) + EPS) * g_ref[...]


def _final_norm(x, g, *, tm):
    t, d = x.shape
    row = pl.BlockSpec((tm, d), lambda i: (i, 0))
    return pl.pallas_call(
        _norm_kernel,
        grid=(t // tm,),
        in_specs=[row, pl.BlockSpec((1, d), lambda i: (0, 0))],
        out_specs=row,
        out_shape=jax.ShapeDtypeStruct((t, d), F32),
        compiler_params=_params("parallel"),
        name="final_norm",
    )(x, g)


def _tile(n, target):
    if n <= target:
        return n
    for c in range(target, 7, -1):
        if n % c == 0 and c % 8 == 0:
            return c
    return n


def _layer_weights(p, l, d, fc):
    f = p["w_ffn1_d"].shape[1]
    nc = f // fc

    def ffn(w_gu, w_d):
        wg = w_gu[l, :, :f].reshape(d, nc, fc).transpose(1, 0, 2).astype(BF16)
        wu = w_gu[l, :, f:].reshape(d, nc, fc).transpose(1, 0, 2).astype(BF16)
        return wg, wu, w_d[l].reshape(nc, fc, d).astype(BF16)

    ng = p["w_in"].shape[2] // d
    return dict(
        ffn1=ffn(p["w_ffn1_gu"], p["w_ffn1_d"]),
        ffn2=ffn(p["w_ffn2_gu"], p["w_ffn2_d"]),
        w_in=p["w_in"][l].reshape(d, ng, d).transpose(1, 0, 2).astype(BF16),
        w_ba=p["w_br_att"][l].astype(BF16),
        w_bh=p["w_br_hg"][l].astype(BF16),
        w_out=p["w_out"][l].astype(BF16),
    )


def _trunk(x3, mods, p, lb_all, weights, cache):
    nb, t, d = x3.shape
    depth = p["w_ada"].shape[0]
    nh = d // LANES
    bt = nb * t
    x = x3.reshape(bt, d)
    per_row = t < 256
    tm = _tile(bt if per_row else t, 512)
    kst = jnp.zeros((depth, bt, d), F32)
    vst = jnp.zeros((depth, bt, d), F32)
    states = []
    for l in range(depth):
        w = weights[l]
        m = mods[l]
        if per_row:
            mm = [jnp.repeat(m[:, i], t, axis=0) for i in range(N_ADA)]
        else:
            mm = [m[:, i].reshape(nb, 1, d) for i in range(N_ADA)]
        sh1, sc1, g1, sh2, sc2, g2, sh3, sc3, g3 = mm
        kw = dict(tm=tm, per_row=per_row, rows_per_seq=t)
        row = lambda a: a.reshape(1, -1)

        x = _ffn(x, row(p["g_ffn1"][l]), sh1, sc1, g1, *w["ffn1"], **kw)
        z, kst, vst = _inproj(x, row(p["g_mix"][l]), sh2, sc2, w["w_in"], kst, vst, l, **kw)

        lam_init = 0.8 - 0.6 * math.exp(-0.3 * l)
        cst = jnp.array([lam_init], F32)
        gsub = row(p["g_att_sub"][l])
        if cache is None:
            tq = _tile(t, 512)
            oa = _attn(cst, p["att_lambda"][l], gsub, z, kst, vst, l,
                       nb=nb, t=t, nh=nh, tq=tq, tk=tq)
            s0 = None
        else:
            ck, cv, s0 = cache
            oa = _attn_cached(cst, p["att_lambda"][l], gsub, z,
                              ck.reshape(*ck.shape[:3], d), cv.reshape(*cv.shape[:3], d),
                              kst, vst, l,
                              nb=nb, t=t, nh=nh)
        tt = _tile(t, 512)
        blk = min(tt, 128)
        oh, s_fin = _hgrn(z, row(lb_all[l]), row(p["g_hg_norm"][l]), s0, l,
                          nb=nb, t=t, nh=nh, tt=tt, blk=blk)
        states.append(s_fin)

        x = _merge(x, g2, oa, oh, z, w["w_ba"], w["w_bh"], w["w_out"], **kw)
        x = _ffn(x, row(p["g_ffn2"][l]), sh3, sc3, g3, *w["ffn2"], **kw)

    y = _final_norm(x, p["g_final"].reshape(1, d), tm=tm).reshape(nb, t, d)
    shape5 = (depth, nb, t, nh, LANES)
    return y, kst.reshape(shape5), vst.reshape(shape5), jnp.stack(states)


def kernel(x_prompt, x_sample, cache_k, cache_v, state_hgrn, c_prompt, c_sample, w_ada, b_ada, g_ffn1, w_ffn1_gu, w_ffn1_d, g_mix, w_in, att_lambda, g_att_sub, hg_lb_logits, g_hg_norm, w_br_att, w_br_hg, w_out, g_ffn2, w_ffn2_gu, w_ffn2_d, g_final):
    p = dict(w_ada=w_ada, b_ada=b_ada, g_ffn1=g_ffn1, w_ffn1_gu=w_ffn1_gu, w_ffn1_d=w_ffn1_d,
             g_mix=g_mix, w_in=w_in, att_lambda=att_lambda, g_att_sub=g_att_sub,
             g_hg_norm=g_hg_norm, w_br_att=w_br_att, w_br_hg=w_br_hg, w_out=w_out,
             g_ffn2=g_ffn2, w_ffn2_gu=w_ffn2_gu, w_ffn2_d=w_ffn2_d, g_final=g_final)
    depth, d, _ = w_ada.shape
    nbp, nbs = x_prompt.shape[0], x_sample.shape[0]

    lb_sm = jax.nn.softmax(hg_lb_logits.astype(F32), axis=0)
    lb_all = jnp.cumsum(lb_sm, axis=0) - lb_sm[0]

    ns = nbp + nbs
    nsp = -(-ns // 8) * 8
    c_all = jnp.concatenate([c_prompt, c_sample, jnp.zeros((nsp - ns, d), F32)], axis=0)
    mods = _ada(c_all, w_ada, b_ada).reshape(depth, nsp, N_ADA, d)

    f = w_ffn1_d.shape[1]
    fc = 256 if f % 256 == 0 else LANES
    weights = [_layer_weights(p, l, d, fc) for l in range(depth)]

    y_p, k_p, v_p, s_p = _trunk(x_prompt, mods[:, :nbp], p, lb_all, weights, None)
    y_s, k_s, v_s, s_s = _trunk(x_sample, mods[:, nbp:ns], p, lb_all, weights,
                                (cache_k, cache_v, state_hgrn))
    return (y_p, y_s, k_p, v_p, s_p, k_s, v_s, s_s)
```

```python
import functools
import math

import jax
import jax.numpy as jnp
from jax import lax
from jax.experimental import pallas as pl
from jax.experimental.pallas import tpu as pltpu

F32 = jnp.float32
BF16 = jnp.bfloat16

CHUNK = 64
N_ADA = 9
MACARON_WEIGHT = 0.5
EPS = 1e-6
MASK_VALUE = -1e30
TINY = 1e-30
LANES = 128
SUB = 16
VMEM_LIMIT = 56 * 1024 * 1024

_NT = (((1,), (1,)), ((), ()))
_TN = (((0,), (0,)), ((), ()))


def _params(*sem):
    return pltpu.CompilerParams(dimension_semantics=sem, vmem_limit_bytes=VMEM_LIMIT)


def _resident(shape):
    nd = len(shape)
    return pl.BlockSpec(shape, lambda *_: (0,) * nd, pipeline_mode=pl.Buffered(1))


def _modnorm(x, g, sc, sh):
    y = x * lax.rsqrt(jnp.mean(x * x, axis=-1, keepdims=True) + EPS) * g
    return y * (1.0 + sc) + sh


def _silu(x):
    return x * jax.nn.sigmoid(x)


def _mod_spec(per_row, tm, d, rows_per_seq):
    if per_row:
        return pl.BlockSpec((tm, d), lambda i, *_: (i, 0))
    tiles = rows_per_seq // tm
    return pl.BlockSpec((None, 1, d), lambda i, *_: (i // tiles, 0, 0))


def _ada_kernel(c_ref, w_ref, b_ref, o_ref):
    c = c_ref[...]
    o_ref[...] = jnp.dot(_silu(c).astype(BF16), w_ref[...].astype(BF16),
                         preferred_element_type=F32) + b_ref[...]


def _ada(c_all, w_ada, b_ada):
    depth, d, _ = w_ada.shape
    ns = c_all.shape[0]
    return pl.pallas_call(
        _ada_kernel,
        grid=(depth, N_ADA),
        in_specs=[pl.BlockSpec((ns, d), lambda l, j: (0, 0)),
                  pl.BlockSpec((None, d, d), lambda l, j: (l, 0, j)),
                  pl.BlockSpec((None, 1, d), lambda l, j: (l, 0, j))],
        out_specs=pl.BlockSpec((None, ns, d), lambda l, j: (l, 0, j)),
        out_shape=jax.ShapeDtypeStruct((depth, ns, N_ADA * d), F32),
        compiler_params=_params("parallel", "parallel"),
        name="ada",
    )(c_all, w_ada, b_ada.reshape(depth, 1, N_ADA * d))


def _ffn_kernel(x_ref, g_ref, sh_ref, sc_ref, gt_ref, wg_ref, wu_ref, wd_ref, o_ref,
                h_scr, acc_scr, *, n_chunks):
    h_scr[...] = _modnorm(x_ref[...], g_ref[...], sc_ref[...], sh_ref[...]).astype(BF16)
    acc_scr[...] = jnp.zeros_like(acc_scr)

    def body(c, carry):
        h = h_scr[...]
        gate = jnp.dot(h, wg_ref[c], preferred_element_type=F32)
        up = jnp.dot(h, wu_ref[c], preferred_element_type=F32)
        a = (_silu(gate) * up).astype(BF16)
        acc_scr[...] += jnp.dot(a, wd_ref[c], preferred_element_type=F32)
        return carry

    lax.fori_loop(0, n_chunks, body, 0)
    o_ref[...] = x_ref[...] + MACARON_WEIGHT * gt_ref[...] * acc_scr[...]


def _ffn(x, g, sh, sc, gt, wg, wu, wd, *, tm, per_row, rows_per_seq):
    t, d = x.shape
    n_chunks = wg.shape[0]
    mod = _mod_spec(per_row, tm, d, rows_per_seq)
    row = pl.BlockSpec((tm, d), lambda i: (i, 0))
    return pl.pallas_call(
        functools.partial(_ffn_kernel, n_chunks=n_chunks),
        grid=(t // tm,),
        in_specs=[row, pl.BlockSpec((1, d), lambda i: (0, 0)), mod, mod, mod,
                  _resident(wg.shape), _resident(wu.shape), _resident(wd.shape)],
        out_specs=row,
        out_shape=jax.ShapeDtypeStruct((t, d), F32),
        scratch_shapes=[pltpu.VMEM((tm, d), BF16), pltpu.VMEM((tm, d), F32)],
        compiler_params=_params("parallel"),
        name="ffn",
    )(x, g, sh, sc, gt, wg, wu, wd)


N_QKV = 3


def _inproj_kernel(x_ref, g_ref, sh_ref, sc_ref, w_ref, kin_ref, vin_ref,
                   z_ref, qkv_ref, k_ref, v_ref, u_scr, *, q_scale):
    del kin_ref, vin_ref
    j = pl.program_id(1)

    @pl.when(j == 0)
    def _():
        u_scr[...] = _modnorm(x_ref[...], g_ref[...], sc_ref[...], sh_ref[...]).astype(BF16)

    def group():
        return jnp.dot(u_scr[...], w_ref[j], preferred_element_type=F32)

    @pl.when(j == 0)
    def _():
        qkv_ref[...] = (group() * q_scale).astype(BF16)

    def cache_rows(ref, z):
        tm, nh = z.shape[0], z.shape[1] // LANES
        for h in range(nh):
            ref[pl.ds(h, tm, stride=nh), :] = z[:, h * LANES:(h + 1) * LANES]

    @pl.when(j == 1)
    def _():
        z = group()
        cache_rows(k_ref, z)
        qkv_ref[...] = z.astype(BF16)

    @pl.when(j == 2)
    def _():
        z = group()
        cache_rows(v_ref, z)
        qkv_ref[...] = z.astype(BF16)

    @pl.when(j >= N_QKV)
    def _():
        z_ref[...] = group()


def _inproj(x, g, sh, sc, w3, kst, vst, layer, *, tm, per_row, rows_per_seq, q_scale):
    t, d = x.shape
    ng = w3.shape[0]
    mod = _mod_spec(per_row, tm, d, rows_per_seq)
    row = pl.BlockSpec((tm, d), lambda i, j: (i, 0))
    stack = pl.BlockSpec((None, tm * (d // LANES), LANES), lambda i, j: (layer, i, 0))
    anyspec = pl.BlockSpec(memory_space=pl.ANY)
    return pl.pallas_call(
        functools.partial(_inproj_kernel, q_scale=q_scale),
        grid=(t // tm, ng),
        in_specs=[row, pl.BlockSpec((1, d), lambda i, j: (0, 0)), mod, mod,
                  _resident(w3.shape), anyspec, anyspec],
        out_specs=[pl.BlockSpec((tm, d), lambda i, j: (i, jnp.maximum(j - N_QKV, 0))),
                   pl.BlockSpec((tm, d), lambda i, j: (i, jnp.minimum(j, N_QKV - 1))),
                   stack, stack],
        out_shape=[jax.ShapeDtypeStruct((t, (ng - N_QKV) * d), F32),
                   jax.ShapeDtypeStruct((t, N_QKV * d), BF16),
                   jax.ShapeDtypeStruct(kst.shape, F32),
                   jax.ShapeDtypeStruct(vst.shape, F32)],
        scratch_shapes=[pltpu.VMEM((tm, d), BF16)],
        input_output_aliases={5: 2, 6: 3},
        compiler_params=_params("parallel", "arbitrary"),
        name="inproj",
    )(x, g, sh, sc, w3, kst, vst)


def _lambda(lamp_ref, lam_init):
    lp = lamp_ref[...]
    s01 = jnp.sum(lp[0:1] * lp[1:2], axis=-1, keepdims=True)
    s23 = jnp.sum(lp[2:3] * lp[3:4], axis=-1, keepdims=True)
    return jnp.exp(s01) - jnp.exp(s23) + lam_init


def _split_maps(q, hd):
    lane = lax.broadcasted_iota(jnp.int32, q.shape, 1)
    zero = jnp.zeros_like(q)
    return jnp.concatenate([jnp.where(lane < hd, q, zero), jnp.where(lane >= hd, q, zero)],
                           axis=0).astype(BF16)


def _visible(rows, cols, tq, q0, k0):
    r = lax.broadcasted_iota(jnp.int32, (rows, cols), 0)
    qpos = q0 + jnp.where(r >= tq, r - tq, r)
    kpos = k0 + lax.broadcasted_iota(jnp.int32, (rows, cols), 1)
    return kpos < (qpos // CHUNK + 1) * CHUNK


def _attn_finish(o1, o2, lam, lam_init, gsub):
    o = o1 - lam * o2
    y = o * lax.rsqrt(jnp.mean(o * o, axis=-1, keepdims=True) + EPS) * gsub
    return y * (1.0 - lam_init)


def _attn_kernel(cst_ref, lamp_ref, gsub_ref, q_ref, k_ref, v_ref, o_ref,
                 qs_scr, vx_scr, m_scr, acc_scr, *, tq, hd):
    i = pl.program_id(2)
    lam_init = cst_ref[0]

    @pl.when(i == 0)
    def _():
        vx_scr[:, :LANES] = v_ref[...]
        vx_scr[:, LANES:] = jnp.ones((vx_scr.shape[0], LANES), BF16)

    q = q_ref[...]
    lane = lax.broadcasted_iota(jnp.int32, q.shape, 1)
    qs_scr[:tq] = jnp.where(lane < hd, q, jnp.zeros_like(q))
    qs_scr[tq:] = jnp.where(lane >= hd, q, jnp.zeros_like(q))
    m_scr[...] = jnp.full_like(m_scr, MASK_VALUE)
    acc_scr[...] = jnp.zeros_like(acc_scr)

    def kv_step(start, width, masked):
        k = k_ref[pl.ds(start, width), :]
        vx = vx_scr[pl.ds(start, width), :]
        if masked:
            vis = _visible(tq, width, tq, i * tq, start)
        maps = (slice(0, tq), slice(tq, 2 * tq))
        scores = [lax.dot_general(qs_scr[rows], k, _NT, preferred_element_type=F32)
                  for rows in maps]
        for rows, s in zip(maps, scores):
            if masked:
                s = jnp.where(vis, s, MASK_VALUE)
            m_old = m_scr[rows]
            m_new = jnp.maximum(m_old, jnp.max(s, axis=-1, keepdims=True))
            p = jnp.exp2(s - jnp.tile(m_new, (1, width // LANES))).astype(BF16)
            alpha = jnp.exp2(m_old - m_new)
            acc_scr[rows] = (jnp.tile(alpha, (1, 2)) * acc_scr[rows]
                             + jnp.dot(p, vx, preferred_element_type=F32))
            m_scr[rows] = m_new

    def full_body(j, carry):
        kv_step(pl.multiple_of(j * 2 * tq, 2 * tq), 2 * tq, False)
        return carry

    lax.fori_loop(0, i // 2, full_body, 0)

    @pl.when(i % 2 == 1)
    def _():
        kv_step(pl.multiple_of((i - 1) * tq, tq), tq, False)

    kv_step(pl.multiple_of(i * tq, tq), tq, True)

    acc = acc_scr[...]
    o12 = acc[:, :LANES] / acc[:, LANES:]
    lam = _lambda(lamp_ref, lam_init)
    o_ref[...] = _attn_finish(o12[:tq], o12[tq:], lam, lam_init, gsub_ref[...])


def _attn(cst, lamp, gsub, qkv, layer, *, nb, t, nh, tq):
    del layer
    bt, d = qkv.shape[0], nh * LANES
    nq = t // tq

    def kv(group):
        return pl.BlockSpec((t, LANES), lambda b, h, i: (b, group * nh + h))

    return pl.pallas_call(
        functools.partial(_attn_kernel, tq=tq, hd=LANES // 2),
        grid=(nb, nh, nq),
        in_specs=[pl.BlockSpec(memory_space=pltpu.SMEM),
                  pl.BlockSpec(lamp.shape, lambda b, h, i: (0, 0)),
                  pl.BlockSpec((1, LANES), lambda b, h, i: (0, 0)),
                  pl.BlockSpec((tq, LANES), lambda b, h, i: (b * nq + i, h)),
                  kv(1), kv(2)],
        out_specs=pl.BlockSpec((tq, LANES), lambda b, h, i: (b * nq + i, h)),
        out_shape=jax.ShapeDtypeStruct((bt, d), F32),
        scratch_shapes=[pltpu.VMEM((2 * tq, LANES), BF16), pltpu.VMEM((t, 2 * LANES), BF16),
                        pltpu.VMEM((2 * tq, LANES), F32), pltpu.VMEM((2 * tq, 2 * LANES), F32)],
        compiler_params=_params("parallel", "parallel", "arbitrary"),
        name="attn",
    )(cst, lamp, gsub, qkv, qkv, qkv)


def _attn_cached_kernel(cst_ref, lamp_ref, gsub_ref, q_ref, kc_ref, vc_ref, kn_ref, vn_ref, o_ref,
                        *, tq, past, hd):
    lam_init = cst_ref[0]
    lam = _lambda(lamp_ref, lam_init)
    nh = q_ref.shape[1] // LANES
    masked = past % CHUNK + tq > CHUNK
    for h in range(nh):
        cols = slice(h * LANES, (h + 1) * LANES)
        head = pl.ds(h, past, stride=nh)
        qs = _split_maps(q_ref[:, cols], hd)

        def scores(k, k0):
            s = lax.dot_general(qs, k.astype(BF16), _NT, preferred_element_type=F32)
            if masked:
                s = jnp.where(_visible(2 * tq, k.shape[0], tq, past, k0), s, MASK_VALUE)
            return s

        sc = scores(kc_ref[head, :], 0)
        sn = scores(kn_ref[:, cols], past)
        m = jnp.maximum(jnp.max(sc, axis=-1, keepdims=True), jnp.max(sn, axis=-1, keepdims=True))
        pc = jnp.exp2(sc - m)
        pn = jnp.exp2(sn - m)
        l = jnp.sum(pc, axis=-1, keepdims=True) + jnp.sum(pn, axis=-1, keepdims=True)
        acc = (jnp.dot(pc.astype(BF16), vc_ref[head, :].astype(BF16), preferred_element_type=F32)
               + jnp.dot(pn.astype(BF16), vn_ref[:, cols], preferred_element_type=F32))
        o12 = acc / l
        o_ref[:, cols] = _attn_finish(o12[:tq], o12[tq:], lam, lam_init, gsub_ref[...])


def _attn_cached(cst, lamp, gsub, qkv, ck, cv, layer, *, nb, t, nh):
    bt, d = qkv.shape[0], nh * LANES
    past = ck.shape[2]
    ck, cv = (c.reshape(c.shape[0], nb, past * nh, LANES) for c in (ck, cv))
    cache = pl.BlockSpec((None, None, past * nh, LANES), lambda b: (layer, b, 0, 0))

    def rows(group):
        return pl.BlockSpec((t, d), lambda b: (b, group))

    return pl.pallas_call(
        functools.partial(_attn_cached_kernel, tq=t, past=past, hd=LANES // 2),
        grid=(nb,),
        in_specs=[pl.BlockSpec(memory_space=pltpu.SMEM),
                  pl.BlockSpec(lamp.shape, lambda b: (0, 0)),
                  pl.BlockSpec((1, LANES), lambda b: (0, 0)),
                  rows(0), cache, cache, rows(1), rows(2)],
        out_specs=pl.BlockSpec((t, d), lambda b: (b, 0)),
        out_shape=jax.ShapeDtypeStruct((bt, d), F32),
        compiler_params=_params("parallel"),
        name="attn_cached",
    )(cst, lamp, gsub, qkv, ck, cv, qkv, qkv)


def _split3(x):
    hi = x.astype(BF16)
    r = x - hi.astype(F32)
    mid = r.astype(BF16)
    lo = (r - mid.astype(F32)).astype(BF16)
    return hi, mid, lo


def _hgrn_kernel(*refs, tt, blk, has_s0):
    hq_ref, hf_ref, hi_ref, hg_ref, lb_ref, gn_ref, tri_ref, ones2_ref = refs[:8]
    s0_ref = refs[8] if has_s0 else None
    o_ref, sf_ref, st_scr, pad_scr, o_scr = refs[-5:]
    t = pl.program_id(2)

    @pl.when(t == 0)
    def _():
        if has_s0:
            st_scr[...] = s0_ref[...].T
        else:
            st_scr[...] = jnp.zeros_like(st_scr)
        pad_scr[:, 0:SUB, :] = jnp.zeros((3, SUB, LANES), F32)

    lb = lb_ref[...]
    sig = jax.nn.sigmoid(hf_ref[...])
    logf = jnp.log(jnp.maximum(lb + (1.0 - lb) * sig, TINY)) * math.log2(math.e)
    kk = (1.0 - lb) * (1.0 - sig)
    qq = _silu(hq_ref[...])
    vv = hi_ref[...]

    tri = tri_ref[...]
    parts = _split3(logf)
    b = jnp.concatenate(
        [sum(jnp.dot(tri, part[c * blk:(c + 1) * blk], preferred_element_type=F32)
             for part in parts) for c in range(tt // blk)], axis=0)

    pad_scr[0, SUB:, :] = kk
    pad_scr[1, SUB:, :] = b
    pad_scr[2, SUB:, :] = vv
    ones2 = ones2_ref[...]
    half = SUB // 2
    n = tt // SUB

    def upper(a):
        return a.reshape(n, 2, half, LANES)[:, 1].reshape(n * half, LANES)

    def lag_terms(d, pick, qsel, bsel, pos, d0):
        ks, bs, vs = (pick(pad_scr[r, SUB - d:SUB - d + tt, :]) for r in range(3))
        db = jnp.where(pos >= d - d0, bsel - bs, MASK_VALUE)
        return (qsel * ks * jnp.exp2(db)).astype(BF16), vs

    def lag_pairs(lags, pick, qsel, bsel, pos, d0):
        acc = jnp.zeros(qsel.shape, F32)
        for d in lags:
            (p0, v0), (p1, v1) = (lag_terms(dd, pick, qsel, bsel, pos, d0) for dd in (d, d + 1))
            r = jnp.dot(jnp.concatenate([p0, p1], axis=1), ones2, preferred_element_type=F32)
            acc = acc + r[:, :LANES] * v0 + r[:, LANES:] * v1
        return acc

    pos_all = lax.broadcasted_iota(jnp.int32, (tt, LANES), 0) % SUB
    pos_up = lax.broadcasted_iota(jnp.int32, (n * half, LANES), 0) % half
    o_all = lag_pairs(range(0, half, 2), lambda a: a, qq, b, pos_all, 0)
    o_up = lag_pairs(range(half, SUB, 2), upper, upper(qq), upper(b), pos_up, half)
    o_up4 = o_up.reshape(n, 1, half, LANES)
    o_scr[...] = o_all + jnp.concatenate([jnp.zeros_like(o_up4), o_up4], axis=1).reshape(tt, LANES)

    m = SUB
    while 2 * m <= blk:
        n = tt // (2 * m)
        b4, q4, k4, v4 = (a.reshape(n, 2, m, LANES) for a in (b, qq, kk, vv))
        b_early, b_late = b4[:, 0], b4[:, 1]
        b_edge = b_early[:, m - 1:m, :]
        qt = (q4[:, 1] * jnp.exp2(b_late - b_edge)).astype(BF16)
        kt = (k4[:, 0] * jnp.exp2(b_edge - b_early)).astype(BF16)
        a = jnp.einsum("nqd,nkd->nqk", qt, kt, preferred_element_type=F32)
        o_late = jnp.einsum("nqk,nkv->nqv", a.astype(BF16), v4[:, 0].astype(BF16),
                            preferred_element_type=F32)
        for g in range(n):
            o_scr[(2 * g + 1) * m:(2 * g + 2) * m, :] += o_late[g]
        m *= 2

    st = st_scr[...]
    for c in range(tt // blk):
        sl = slice(c * blk, (c + 1) * blk)
        bc = b[sl]
        b_last = bc[blk - 1:blk]
        qt = (qq[sl] * jnp.exp2(bc)).astype(BF16)
        o_scr[sl, :] += lax.dot_general(qt, st.astype(BF16), _NT, preferred_element_type=F32)
        kt = (kk[sl] * jnp.exp2(b_last - bc)).astype(BF16)
        st = st * jnp.exp2(b_last) + lax.dot_general(vv[sl].astype(BF16), kt, _TN,
                                                    preferred_element_type=F32)
    st_scr[...] = st

    o = o_scr[...]
    y = o * lax.rsqrt(jnp.mean(o * o, axis=-1, keepdims=True) + EPS) * gn_ref[...]
    o_ref[...] = y * _silu(hg_ref[...])

    @pl.when(t == pl.num_programs(2) - 1)
    def _():
        sf_ref[...] = st.T


def _hgrn(z, lb, gn, s0, layer, *, nb, t, nh, tt, blk):
    bt = z.shape[0]
    nt = t // tt
    has_s0 = s0 is not None

    def col(group):
        return pl.BlockSpec((tt, LANES), lambda b, h, i: (b * nt + i, group * nh + h))

    state = pl.BlockSpec((None, None, LANES, LANES), lambda b, h, i: (b, h, 0, 0))
    tri = jnp.tril(jnp.ones((blk, blk), BF16))
    ones2 = jnp.kron(jnp.eye(2, dtype=BF16), jnp.ones((LANES, LANES), BF16))
    in_specs = [col(0), col(1), col(2), col(3),
                pl.BlockSpec((1, LANES), lambda b, h, i: (0, h)),
                pl.BlockSpec((1, LANES), lambda b, h, i: (0, 0)),
                _resident(tri.shape), _resident(ones2.shape)]
    args = [z, z, z, z, lb, gn, tri, ones2]
    if has_s0:
        in_specs.append(pl.BlockSpec((None, None, None, LANES, LANES),
                                     lambda b, h, i: (layer, b, h, 0, 0)))
        args.append(s0)
    return pl.pallas_call(
        functools.partial(_hgrn_kernel, tt=tt, blk=blk, has_s0=has_s0),
        grid=(nb, nh, nt),
        in_specs=in_specs,
        out_specs=[pl.BlockSpec((tt, LANES), lambda b, h, i: (b * nt + i, h)), state],
        out_shape=[jax.ShapeDtypeStruct((bt, nh * LANES), F32),
                   jax.ShapeDtypeStruct((nb, nh, LANES, LANES), F32)],
        scratch_shapes=[pltpu.VMEM((LANES, LANES), F32), pltpu.VMEM((3, SUB + tt, LANES), F32),
                        pltpu.VMEM((tt, LANES), F32)],
        compiler_params=_params("parallel", "parallel", "arbitrary"),
        name="hgrn",
    )(*args)


def _merge_kernel(x_ref, gt_ref, oa_ref, oh_ref, za_ref, zh_ref, wa_ref, wh_ref, wo_ref, o_ref):
    a = jnp.dot(oa_ref[...].astype(BF16), wa_ref[...], preferred_element_type=F32)
    r = jnp.dot(oh_ref[...].astype(BF16), wh_ref[...], preferred_element_type=F32)
    merged = jax.nn.sigmoid(za_ref[...]) * a + jax.nn.sigmoid(zh_ref[...]) * r
    o_ref[...] = x_ref[...] + gt_ref[...] * jnp.dot(merged.astype(BF16), wo_ref[...],
                                                    preferred_element_type=F32)


def _merge(x, gt, oa, oh, z, wa, wh, wo, *, tm, per_row, rows_per_seq):
    t, d = x.shape
    mod = _mod_spec(per_row, tm, d, rows_per_seq)
    row = pl.BlockSpec((tm, d), lambda i: (i, 0))
    return pl.pallas_call(
        _merge_kernel,
        grid=(t // tm,),
        in_specs=[row, mod, row, row,
                  pl.BlockSpec((tm, d), lambda i: (i, 4)), pl.BlockSpec((tm, d), lambda i: (i, 5)),
                  _resident(wa.shape), _resident(wh.shape), _resident(wo.shape)],
        out_specs=row,
        out_shape=jax.ShapeDtypeStruct((t, d), F32),
        compiler_params=_params("parallel"),
        name="merge",
    )(x, gt, oa, oh, z, z, wa, wh, wo)


def _norm_kernel(x_ref, g_ref, o_ref):
    x = x_ref[...]
    o_ref[...] = x * lax.rsqrt(jnp.mean(x * x, axis=-1, keepdims=True) + EPS) * g_ref[...]


def _final_norm(x, g, *, tm):
    t, d = x.shape
    row = pl.BlockSpec((tm, d), lambda i: (i, 0))
    return pl.pallas_call(
        _norm_kernel,
        grid=(t // tm,),
        in_specs=[row, pl.BlockSpec((1, d), lambda i: (0, 0))],
        out_specs=row,
        out_shape=jax.ShapeDtypeStruct((t, d), F32),
        compiler_params=_params("parallel"),
        name="final_norm",
    )(x, g)


def _tile(n, target):
    if n <= target:
        return n
    for c in range(target, 7, -1):
        if n % c == 0 and c % 8 == 0:
            return c
    return n


def _layer_weights(p, l, d, fc):
    f = p["w_ffn1_d"].shape[1]
    nc = f // fc

    def ffn(w_gu, w_d):
        wg = w_gu[l, :, :f].reshape(d, nc, fc).transpose(1, 0, 2).astype(BF16)
        wu = w_gu[l, :, f:].reshape(d, nc, fc).transpose(1, 0, 2).astype(BF16)
        return wg, wu, w_d[l].reshape(nc, fc, d).astype(BF16)

    ng = p["w_in"].shape[2] // d
    return dict(
        ffn1=ffn(p["w_ffn1_gu"], p["w_ffn1_d"]),
        ffn2=ffn(p["w_ffn2_gu"], p["w_ffn2_d"]),
        w_in=p["w_in"][l].reshape(d, ng, d).transpose(1, 0, 2).astype(BF16),
        w_ba=p["w_br_att"][l].astype(BF16),
        w_bh=p["w_br_hg"][l].astype(BF16),
        w_out=p["w_out"][l].astype(BF16),
    )


def _trunk(x3, mods, p, lb_all, weights, cache):
    nb, t, d = x3.shape
    depth = p["w_ada"].shape[0]
    nh = d // LANES
    bt = nb * t
    x = x3.reshape(bt, d)
    per_row = t < 256
    tm = _tile(bt if per_row else t, 512)
    kst = jnp.zeros((depth, bt * nh, LANES), F32)
    vst = jnp.zeros((depth, bt * nh, LANES), F32)
    states = []
    for l in range(depth):
        w = weights[l]
        m = mods[l]
        if per_row:
            mm = [jnp.repeat(m[:, i], t, axis=0) for i in range(N_ADA)]
        else:
            mm = [m[:, i].reshape(nb, 1, d) for i in range(N_ADA)]
        sh1, sc1, g1, sh2, sc2, g2, sh3, sc3, g3 = mm
        kw = dict(tm=tm, per_row=per_row, rows_per_seq=t)
        row = lambda a: a.reshape(1, -1)

        x = _ffn(x, row(p["g_ffn1"][l]), sh1, sc1, g1, *w["ffn1"], **kw)
        q_scale = (LANES // 2) ** -0.5 * math.log2(math.e)
        z, qkv, kst, vst = _inproj(x, row(p["g_mix"][l]), sh2, sc2, w["w_in"], kst, vst, l,
                                   q_scale=q_scale, **kw)

        lam_init = 0.8 - 0.6 * math.exp(-0.3 * l)
        cst = jnp.array([lam_init], F32)
        gsub = row(p["g_att_sub"][l])
        if cache is None:
            oa = _attn(cst, p["att_lambda"][l], gsub, qkv, l, nb=nb, t=t, nh=nh, tq=_tile(t, 512))
            s0 = None
        else:
            ck, cv, s0 = cache
            oa = _attn_cached(cst, p["att_lambda"][l], gsub, qkv, ck, cv, l, nb=nb, t=t, nh=nh)
        tt = _tile(t, 512)
        blk = min(tt, 128)
        oh, s_fin = _hgrn(z, row(lb_all[l]), row(p["g_hg_norm"][l]), s0, l,
                          nb=nb, t=t, nh=nh, tt=tt, blk=blk)
        states.append(s_fin)

        x = _merge(x, g2, oa, oh, z, w["w_ba"], w["w_bh"], w["w_out"], **kw)
        x = _ffn(x, row(p["g_ffn2"][l]), sh3, sc3, g3, *w["ffn2"], **kw)

    y = _final_norm(x, p["g_final"].reshape(1, d), tm=tm).reshape(nb, t, d)
    shape5 = (depth, nb, t, nh, LANES)
    return y, kst.reshape(shape5), vst.reshape(shape5), jnp.stack(states)


def kernel(x_prompt, x_sample, cache_k, cache_v, state_hgrn, c_prompt, c_sample, w_ada, b_ada, g_ffn1, w_ffn1_gu, w_ffn1_d, g_mix, w_in, att_lambda, g_att_sub, hg_lb_logits, g_hg_norm, w_br_att, w_br_hg, w_out, g_ffn2, w_ffn2_gu, w_ffn2_d, g_final):
    p = dict(w_ada=w_ada, b_ada=b_ada, g_ffn1=g_ffn1, w_ffn1_gu=w_ffn1_gu, w_ffn1_d=w_ffn1_d,
             g_mix=g_mix, w_in=w_in, att_lambda=att_lambda, g_att_sub=g_att_sub,
             g_hg_norm=g_hg_norm, w_br_att=w_br_att, w_br_hg=w_br_hg, w_out=w_out,
             g_ffn2=g_ffn2, w_ffn2_gu=w_ffn2_gu, w_ffn2_d=w_ffn2_d, g_final=g_final)
    depth, d, _ = w_ada.shape
    nbp, nbs = x_prompt.shape[0], x_sample.shape[0]

    lb_sm = jax.nn.softmax(hg_lb_logits.astype(F32), axis=0)
    lb_all = jnp.cumsum(lb_sm, axis=0) - lb_sm[0]

    ns = nbp + nbs
    nsp = -(-ns // 8) * 8
    c_all = jnp.concatenate([c_prompt, c_sample, jnp.zeros((nsp - ns, d), F32)], axis=0)
    mods = _ada(c_all, w_ada, b_ada).reshape(depth, nsp, N_ADA, d)

    f = w_ffn1_d.shape[1]
    fc = 256 if f % 256 == 0 else LANES
    weights = [_layer_weights(p, l, d, fc) for l in range(depth)]

    y_p, k_p, v_p, s_p = _trunk(x_prompt, mods[:, :nbp], p, lb_all, weights, None)
    y_s, k_s, v_s, s_s = _trunk(x_sample, mods[:, nbp:ns], p, lb_all, weights,
                                (cache_k, cache_v, state_hgrn))
    return (y_p, y_s, k_p, v_p, s_p, k_s, v_s, s_s)
```

```python
import functools
import math

import jax
import jax.numpy as jnp
from jax import lax
from jax.experimental import pallas as pl
from jax.experimental.pallas import tpu as pltpu

F32 = jnp.float32
BF16 = jnp.bfloat16

CHUNK = 64
N_ADA = 9
MACARON_WEIGHT = 0.5
EPS = 1e-6
MASK_VALUE = -1e30
TINY = 1e-30
LANES = 128
SUB = 16
VMEM_LIMIT = 56 * 1024 * 1024

_NT = (((1,), (1,)), ((), ()))
_TN = (((0,), (0,)), ((), ()))


def _params(*sem):
    return pltpu.CompilerParams(dimension_semantics=sem, vmem_limit_bytes=VMEM_LIMIT)


def _resident(shape):
    nd = len(shape)
    return pl.BlockSpec(shape, lambda *_: (0,) * nd, pipeline_mode=pl.Buffered(1))


def _modnorm(x, g, sc, sh):
    y = x * lax.rsqrt(jnp.mean(x * x, axis=-1, keepdims=True) + EPS) * g
    return y * (1.0 + sc) + sh


def _silu(x):
    return x * jax.nn.sigmoid(x)


def _mod_spec(per_row, tm, d, rows_per_seq):
    if per_row:
        return pl.BlockSpec((tm, d), lambda i, *_: (i, 0))
    tiles = rows_per_seq // tm
    return pl.BlockSpec((None, 1, d), lambda i, *_: (i // tiles, 0, 0))


def _ada_kernel(c_ref, w_ref, b_ref, o_ref):
    c = c_ref[...]
    o_ref[...] = jnp.dot(_silu(c).astype(BF16), w_ref[...].astype(BF16),
                         preferred_element_type=F32) + b_ref[...]


def _ada(c_all, w_ada, b_ada):
    depth, d, _ = w_ada.shape
    ns = c_all.shape[0]
    return pl.pallas_call(
        _ada_kernel,
        grid=(depth, N_ADA),
        in_specs=[pl.BlockSpec((ns, d), lambda l, j: (0, 0)),
                  pl.BlockSpec((None, d, d), lambda l, j: (l, 0, j)),
                  pl.BlockSpec((None, 1, d), lambda l, j: (l, 0, j))],
        out_specs=pl.BlockSpec((None, ns, d), lambda l, j: (l, 0, j)),
        out_shape=jax.ShapeDtypeStruct((depth, ns, N_ADA * d), F32),
        compiler_params=_params("parallel", "parallel"),
        name="ada",
    )(c_all, w_ada, b_ada.reshape(depth, 1, N_ADA * d))


def _ffn_kernel(x_ref, g_ref, sh_ref, sc_ref, gt_ref, wg_ref, wu_ref, wd_ref, o_ref,
                h_scr, acc_scr, *, n_chunks):
    h_scr[...] = _modnorm(x_ref[...], g_ref[...], sc_ref[...], sh_ref[...]).astype(BF16)
    for c in range(n_chunks):
        h = h_scr[...]
        gate = jnp.dot(h, wg_ref[c], preferred_element_type=F32)
        up = jnp.dot(h, wu_ref[c], preferred_element_type=F32)
        a = (_silu(gate) * up).astype(BF16)
        down = jnp.dot(a, wd_ref[c], preferred_element_type=F32)
        if c == 0:
            acc_scr[...] = down
        else:
            acc_scr[...] += down
    o_ref[...] = x_ref[...] + MACARON_WEIGHT * gt_ref[...] * acc_scr[...]


def _ffn(x, g, sh, sc, gt, wg, wu, wd, *, tm, per_row, rows_per_seq):
    t, d = x.shape
    n_chunks = wg.shape[0]
    mod = _mod_spec(per_row, tm, d, rows_per_seq)
    row = pl.BlockSpec((tm, d), lambda i: (i, 0))
    return pl.pallas_call(
        functools.partial(_ffn_kernel, n_chunks=n_chunks),
        grid=(t // tm,),
        in_specs=[row, pl.BlockSpec((1, d), lambda i: (0, 0)), mod, mod, mod,
                  _resident(wg.shape), _resident(wu.shape), _resident(wd.shape)],
        out_specs=row,
        out_shape=jax.ShapeDtypeStruct((t, d), F32),
        scratch_shapes=[pltpu.VMEM((tm, d), BF16), pltpu.VMEM((tm, d), F32)],
        compiler_params=_params("parallel"),
        name="ffn",
    )(x, g, sh, sc, gt, wg, wu, wd)


N_QKV = 3


def _inproj_kernel(x_ref, g_ref, sh_ref, sc_ref, w_ref, kin_ref, vin_ref,
                   z_ref, qkv_ref, k_ref, v_ref, u_scr, *, q_scale):
    del kin_ref, vin_ref
    j = pl.program_id(1)
    d = u_scr.shape[1]

    def group(g):
        return jnp.dot(u_scr[...], w_ref[g], preferred_element_type=F32)

    def cache_rows(ref, z):
        tm, nh = z.shape[0], z.shape[1] // LANES
        for h in range(nh):
            ref[pl.ds(h, tm, stride=nh), :] = z[:, h * LANES:(h + 1) * LANES]

    @pl.when(j == 0)
    def _():
        u_scr[...] = _modnorm(x_ref[...], g_ref[...], sc_ref[...], sh_ref[...]).astype(BF16)
        qkv_ref[:, 0:d] = (group(0) * q_scale).astype(BF16)
        for g, ref in ((1, k_ref), (2, v_ref)):
            z = group(g)
            cache_rows(ref, z)
            qkv_ref[:, g * d:(g + 1) * d] = z.astype(BF16)

    @pl.when(j > 0)
    def _():
        for g in range(N_QKV):
            z_ref[:, g * d:(g + 1) * d] = group(N_QKV * j + g)


def _inproj(x, g, sh, sc, w3, kst, vst, layer, *, tm, per_row, rows_per_seq, q_scale):
    t, d = x.shape
    ng = w3.shape[0]
    mod = _mod_spec(per_row, tm, d, rows_per_seq)
    row = pl.BlockSpec((tm, d), lambda i, j: (i, 0))
    stack = pl.BlockSpec((None, tm * (d // LANES), LANES), lambda i, j: (layer, i, 0))
    anyspec = pl.BlockSpec(memory_space=pl.ANY)
    return pl.pallas_call(
        functools.partial(_inproj_kernel, q_scale=q_scale),
        grid=(t // tm, ng // N_QKV),
        in_specs=[row, pl.BlockSpec((1, d), lambda i, j: (0, 0)), mod, mod,
                  _resident(w3.shape), anyspec, anyspec],
        out_specs=[pl.BlockSpec((tm, N_QKV * d), lambda i, j: (i, jnp.maximum(j - 1, 0))),
                   pl.BlockSpec((tm, N_QKV * d), lambda i, j: (i, 0)),
                   stack, stack],
        out_shape=[jax.ShapeDtypeStruct((t, (ng - N_QKV) * d), F32),
                   jax.ShapeDtypeStruct((t, N_QKV * d), BF16),
                   jax.ShapeDtypeStruct(kst.shape, F32),
                   jax.ShapeDtypeStruct(vst.shape, F32)],
        scratch_shapes=[pltpu.VMEM((tm, d), BF16)],
        input_output_aliases={5: 2, 6: 3},
        compiler_params=_params("parallel", "arbitrary"),
        name="inproj",
    )(x, g, sh, sc, w3, kst, vst)


def _lambda(lamp_ref, lam_init):
    lp = lamp_ref[...]
    s01 = jnp.sum(lp[0:1] * lp[1:2], axis=-1, keepdims=True)
    s23 = jnp.sum(lp[2:3] * lp[3:4], axis=-1, keepdims=True)
    return jnp.exp(s01) - jnp.exp(s23) + lam_init


def _split_maps(q, hd):
    lane = lax.broadcasted_iota(jnp.int32, q.shape, 1)
    zero = jnp.zeros_like(q)
    return jnp.concatenate([jnp.where(lane < hd, q, zero), jnp.where(lane >= hd, q, zero)],
                           axis=0).astype(BF16)


def _visible(rows, cols, tq, q0, k0):
    r = lax.broadcasted_iota(jnp.int32, (rows, cols), 0)
    qpos = q0 + jnp.where(r >= tq, r - tq, r)
    kpos = k0 + lax.broadcasted_iota(jnp.int32, (rows, cols), 1)
    return kpos < (qpos // CHUNK + 1) * CHUNK


def _attn_finish(o1, o2, lam, lam_init, gsub):
    o = o1 - lam * o2
    y = o * lax.rsqrt(jnp.mean(o * o, axis=-1, keepdims=True) + EPS) * gsub
    return y * (1.0 - lam_init)


def _attn_kernel(cst_ref, lamp_ref, gsub_ref, q_ref, k_ref, v_ref, o_ref,
                 qs_scr, vx_scr, s_scr, m_scr, acc_scr, *, tq, hd):
    i = pl.program_id(2)
    lam_init = cst_ref[0]

    @pl.when(i == 0)
    def _():
        vx_scr[:, :LANES] = v_ref[...]
        vx_scr[:, LANES:] = jnp.ones((vx_scr.shape[0], LANES), BF16)

    q = q_ref[...]
    lane = lax.broadcasted_iota(jnp.int32, q.shape, 1)
    qs_scr[:tq] = jnp.where(lane < hd, q, jnp.zeros_like(q))
    qs_scr[tq:] = jnp.where(lane >= hd, q, jnp.zeros_like(q))
    m_scr[...] = jnp.full_like(m_scr, MASK_VALUE)
    acc_scr[...] = jnp.zeros_like(acc_scr)

    width = tq
    maps = (slice(0, tq), slice(tq, 2 * tq))

    def tile_scores(j, slot):
        k = k_ref[pl.ds(pl.multiple_of(j * width, width), width), :]
        for rows in maps:
            s_scr[slot, rows, :] = lax.dot_general(qs_scr[rows], k, _NT,
                                                   preferred_element_type=F32)

    def tile_update(j, slot, masked):
        start = pl.multiple_of(j * width, width)
        vx = vx_scr[pl.ds(start, width), :]
        if masked:
            vis = _visible(tq, width, tq, i * tq, start)
        for rows in maps:
            s = s_scr[slot, rows, :]
            if masked:
                s = jnp.where(vis, s, MASK_VALUE)
            m_old = m_scr[rows]
            m_new = jnp.maximum(m_old, jnp.max(s, axis=-1, keepdims=True))
            p = jnp.exp2(s - jnp.tile(m_new, (1, width // LANES))).astype(BF16)
            alpha = jnp.exp2(m_old - m_new)
            acc_scr[rows] = (jnp.tile(alpha, (1, 2)) * acc_scr[rows]
                             + jnp.dot(p, vx, preferred_element_type=F32))
            m_scr[rows] = m_new

    last = i
    tile_scores(0, 0)

    def body(k, carry):
        j = 2 * k
        tile_scores(j + 1, 1)
        tile_update(j, 0, False)
        tile_scores(j + 2, 0)
        tile_update(j + 1, 1, False)
        return carry

    lax.fori_loop(0, last // 2, body, 0)

    @pl.when(last % 2 == 1)
    def _():
        tile_scores(last, 1)
        tile_update(last - 1, 0, False)
        tile_update(last, 1, True)

    @pl.when(last % 2 == 0)
    def _():
        tile_update(last, 0, True)

    acc = acc_scr[...]
    o12 = acc[:, :LANES] / acc[:, LANES:]
    lam = _lambda(lamp_ref, lam_init)
    o_ref[...] = _attn_finish(o12[:tq], o12[tq:], lam, lam_init, gsub_ref[...])


def _attn(cst, lamp, gsub, qkv, layer, *, nb, t, nh, tq):
    del layer
    bt, d = qkv.shape[0], nh * LANES
    nq = t // tq

    def kv(group):
        return pl.BlockSpec((t, LANES), lambda b, h, i: (b, group * nh + h))

    return pl.pallas_call(
        functools.partial(_attn_kernel, tq=tq, hd=LANES // 2),
        grid=(nb, nh, nq),
        in_specs=[pl.BlockSpec(memory_space=pltpu.SMEM),
                  pl.BlockSpec(lamp.shape, lambda b, h, i: (0, 0)),
                  pl.BlockSpec((1, LANES), lambda b, h, i: (0, 0)),
                  pl.BlockSpec((tq, LANES), lambda b, h, i: (b * nq + i, h)),
                  kv(1), kv(2)],
        out_specs=pl.BlockSpec((tq, LANES), lambda b, h, i: (b * nq + i, h)),
        out_shape=jax.ShapeDtypeStruct((bt, d), F32),
        scratch_shapes=[pltpu.VMEM((2 * tq, LANES), BF16), pltpu.VMEM((t, 2 * LANES), BF16),
                        pltpu.VMEM((2, 2 * tq, tq), F32),
                        pltpu.VMEM((2 * tq, LANES), F32), pltpu.VMEM((2 * tq, 2 * LANES), F32)],
        compiler_params=_params("parallel", "parallel", "arbitrary"),
        name="attn",
    )(cst, lamp, gsub, qkv, qkv, qkv)


def _attn_cached_kernel(cst_ref, lamp_ref, gsub_ref, q_ref, kc_ref, vc_ref, kn_ref, vn_ref, o_ref,
                        *, tq, past, hd):
    lam_init = cst_ref[0]
    lam = _lambda(lamp_ref, lam_init)
    nh = q_ref.shape[1] // LANES
    masked = past % CHUNK + tq > CHUNK
    for h in range(nh):
        cols = slice(h * LANES, (h + 1) * LANES)
        head = pl.ds(h, past, stride=nh)
        qs = _split_maps(q_ref[:, cols], hd)

        def scores(k, k0):
            s = lax.dot_general(qs, k.astype(BF16), _NT, preferred_element_type=F32)
            if masked:
                s = jnp.where(_visible(2 * tq, k.shape[0], tq, past, k0), s, MASK_VALUE)
            return s

        sc = scores(kc_ref[head, :], 0)
        sn = scores(kn_ref[:, cols], past)
        m = jnp.maximum(jnp.max(sc, axis=-1, keepdims=True), jnp.max(sn, axis=-1, keepdims=True))
        pc = jnp.exp2(sc - m)
        pn = jnp.exp2(sn - m)
        l = jnp.sum(pc, axis=-1, keepdims=True) + jnp.sum(pn, axis=-1, keepdims=True)
        acc = (jnp.dot(pc.astype(BF16), vc_ref[head, :].astype(BF16), preferred_element_type=F32)
               + jnp.dot(pn.astype(BF16), vn_ref[:, cols], preferred_element_type=F32))
        o12 = acc / l
        o_ref[:, cols] = _attn_finish(o12[:tq], o12[tq:], lam, lam_init, gsub_ref[...])


def _attn_cached(cst, lamp, gsub, qkv, ck, cv, layer, *, nb, t, nh):
    bt, d = qkv.shape[0], nh * LANES
    past = ck.shape[2]
    ck, cv = (c.reshape(c.shape[0], nb, past * nh, LANES) for c in (ck, cv))
    cache = pl.BlockSpec((None, None, past * nh, LANES), lambda b: (layer, b, 0, 0))

    def rows(group):
        return pl.BlockSpec((t, d), lambda b: (b, group))

    return pl.pallas_call(
        functools.partial(_attn_cached_kernel, tq=t, past=past, hd=LANES // 2),
        grid=(nb,),
        in_specs=[pl.BlockSpec(memory_space=pltpu.SMEM),
                  pl.BlockSpec(lamp.shape, lambda b: (0, 0)),
                  pl.BlockSpec((1, LANES), lambda b: (0, 0)),
                  rows(0), cache, cache, rows(1), rows(2)],
        out_specs=pl.BlockSpec((t, d), lambda b: (b, 0)),
        out_shape=jax.ShapeDtypeStruct((bt, d), F32),
        compiler_params=_params("parallel"),
        name="attn_cached",
    )(cst, lamp, gsub, qkv, ck, cv, qkv, qkv)


def _split3(x):
    hi = x.astype(BF16)
    r = x - hi.astype(F32)
    mid = r.astype(BF16)
    lo = (r - mid.astype(F32)).astype(BF16)
    return hi, mid, lo


def _hgrn_kernel(*refs, tt, blk, has_s0):
    hq_ref, hf_ref, hi_ref, hg_ref, lb_ref, gn_ref, tri_ref, ones2_ref = refs[:8]
    s0_ref = refs[8] if has_s0 else None
    o_ref, sf_ref, st_scr, pad_scr, o_scr = refs[-5:]
    t = pl.program_id(2)

    @pl.when(t == 0)
    def _():
        if has_s0:
            st_scr[...] = s0_ref[...].T
        else:
            st_scr[...] = jnp.zeros_like(st_scr)
        pad_scr[:, 0:SUB, :] = jnp.zeros((3, SUB, LANES), F32)

    lb = lb_ref[...]
    sig = jax.nn.sigmoid(hf_ref[...])
    logf = jnp.log(jnp.maximum(lb + (1.0 - lb) * sig, TINY)) * math.log2(math.e)
    kk = (1.0 - lb) * (1.0 - sig)
    qq = _silu(hq_ref[...])
    vv = hi_ref[...]

    tri = tri_ref[...]
    parts = _split3(logf)
    b = jnp.concatenate(
        [sum(jnp.dot(tri, part[c * blk:(c + 1) * blk], preferred_element_type=F32)
             for part in parts) for c in range(tt // blk)], axis=0)

    pad_scr[0, SUB:, :] = kk
    pad_scr[1, SUB:, :] = b
    pad_scr[2, SUB:, :] = vv
    ones2 = ones2_ref[...]
    half = SUB // 2
    n = tt // SUB

    def upper(a):
        return a.reshape(n, 2, half, LANES)[:, 1].reshape(n * half, LANES)

    def lag_terms(d, pick, qsel, bsel, pos, d0):
        ks, bs, vs = (pick(pad_scr[r, SUB - d:SUB - d + tt, :]) for r in range(3))
        db = jnp.where(pos >= d - d0, bsel - bs, MASK_VALUE)
        return (qsel * ks * jnp.exp2(db)).astype(BF16), vs

    def lag_pairs(lags, pick, qsel, bsel, pos, d0):
        acc = jnp.zeros(qsel.shape, F32)
        for d in lags:
            (p0, v0), (p1, v1) = (lag_terms(dd, pick, qsel, bsel, pos, d0) for dd in (d, d + 1))
            r = jnp.dot(jnp.concatenate([p0, p1], axis=1), ones2, preferred_element_type=F32)
            acc = acc + r[:, :LANES] * v0 + r[:, LANES:] * v1
        return acc

    pos_all = lax.broadcasted_iota(jnp.int32, (tt, LANES), 0) % SUB
    pos_up = lax.broadcasted_iota(jnp.int32, (n * half, LANES), 0) % half
    o_all = lag_pairs(range(0, half, 2), lambda a: a, qq, b, pos_all, 0)
    o_up = lag_pairs(range(half, SUB, 2), upper, upper(qq), upper(b), pos_up, half)
    o_up4 = o_up.reshape(n, 1, half, LANES)
    o_scr[...] = o_all + jnp.concatenate([jnp.zeros_like(o_up4), o_up4], axis=1).reshape(tt, LANES)

    m = SUB
    while 2 * m <= blk:
        n = tt // (2 * m)
        b4, q4, k4, v4 = (a.reshape(n, 2, m, LANES) for a in (b, qq, kk, vv))
        b_early, b_late = b4[:, 0], b4[:, 1]
        b_edge = b_early[:, m - 1:m, :]
        qt = (q4[:, 1] * jnp.exp2(b_late - b_edge)).astype(BF16)
        kt = (k4[:, 0] * jnp.exp2(b_edge - b_early)).astype(BF16)
        a = jnp.einsum("nqd,nkd->nqk", qt, kt, preferred_element_type=F32)
        o_late = jnp.einsum("nqk,nkv->nqv", a.astype(BF16), v4[:, 0].astype(BF16),
                            preferred_element_type=F32)
        for g in range(n):
            o_scr[(2 * g + 1) * m:(2 * g + 2) * m, :] += o_late[g]
        m *= 2

    st = st_scr[...]
    for c in range(tt // blk):
        sl = slice(c * blk, (c + 1) * blk)
        bc = b[sl]
        b_last = bc[blk - 1:blk]
        qt = (qq[sl] * jnp.exp2(bc)).astype(BF16)
        o_scr[sl, :] += lax.dot_general(qt, st.astype(BF16), _NT, preferred_element_type=F32)
        kt = (kk[sl] * jnp.exp2(b_last - bc)).astype(BF16)
        st = st * jnp.exp2(b_last) + lax.dot_general(vv[sl].astype(BF16), kt, _TN,
                                                    preferred_element_type=F32)
    st_scr[...] = st

    o = o_scr[...]
    y = o * lax.rsqrt(jnp.mean(o * o, axis=-1, keepdims=True) + EPS) * gn_ref[...]
    o_ref[...] = y * _silu(hg_ref[...])

    @pl.when(t == pl.num_programs(2) - 1)
    def _():
        sf_ref[...] = st.T


def _hgrn(z, lb, gn, s0, layer, *, nb, t, nh, tt, blk):
    bt = z.shape[0]
    nt = t // tt
    has_s0 = s0 is not None

    def col(group):
        return pl.BlockSpec((tt, LANES), lambda b, h, i: (b * nt + i, group * nh + h))

    state = pl.BlockSpec((None, None, LANES, LANES), lambda b, h, i: (b, h, 0, 0))
    tri = jnp.tril(jnp.ones((blk, blk), BF16))
    ones2 = jnp.kron(jnp.eye(2, dtype=BF16), jnp.ones((LANES, LANES), BF16))
    in_specs = [col(0), col(1), col(2), col(3),
                pl.BlockSpec((1, LANES), lambda b, h, i: (0, h)),
                pl.BlockSpec((1, LANES), lambda b, h, i: (0, 0)),
                _resident(tri.shape), _resident(ones2.shape)]
    args = [z, z, z, z, lb, gn, tri, ones2]
    if has_s0:
        in_specs.append(pl.BlockSpec((None, None, None, LANES, LANES),
                                     lambda b, h, i: (layer, b, h, 0, 0)))
        args.append(s0)
    return pl.pallas_call(
        functools.partial(_hgrn_kernel, tt=tt, blk=blk, has_s0=has_s0),
        grid=(nb, nh, nt),
        in_specs=in_specs,
        out_specs=[pl.BlockSpec((tt, LANES), lambda b, h, i: (b * nt + i, h)), state],
        out_shape=[jax.ShapeDtypeStruct((bt, nh * LANES), F32),
                   jax.ShapeDtypeStruct((nb, nh, LANES, LANES), F32)],
        scratch_shapes=[pltpu.VMEM((LANES, LANES), F32), pltpu.VMEM((3, SUB + tt, LANES), F32),
                        pltpu.VMEM((tt, LANES), F32)],
        compiler_params=_params("parallel", "parallel", "arbitrary"),
        name="hgrn",
    )(*args)


def _merge_kernel(x_ref, gt_ref, oa_ref, oh_ref, za_ref, zh_ref, wa_ref, wh_ref, wo_ref, o_ref):
    a = jnp.dot(oa_ref[...].astype(BF16), wa_ref[...], preferred_element_type=F32)
    r = jnp.dot(oh_ref[...].astype(BF16), wh_ref[...], preferred_element_type=F32)
    merged = jax.nn.sigmoid(za_ref[...]) * a + jax.nn.sigmoid(zh_ref[...]) * r
    o_ref[...] = x_ref[...] + gt_ref[...] * jnp.dot(merged.astype(BF16), wo_ref[...],
                                                    preferred_element_type=F32)


def _merge(x, gt, oa, oh, z, wa, wh, wo, *, tm, per_row, rows_per_seq):
    t, d = x.shape
    mod = _mod_spec(per_row, tm, d, rows_per_seq)
    row = pl.BlockSpec((tm, d), lambda i: (i, 0))
    return pl.pallas_call(
        _merge_kernel,
        grid=(t // tm,),
        in_specs=[row, mod, row, row,
                  pl.BlockSpec((tm, d), lambda i: (i, 4)), pl.BlockSpec((tm, d), lambda i: (i, 5)),
                  _resident(wa.shape), _resident(wh.shape), _resident(wo.shape)],
        out_specs=row,
        out_shape=jax.ShapeDtypeStruct((t, d), F32),
        compiler_params=_params("parallel"),
        name="merge",
    )(x, gt, oa, oh, z, z, wa, wh, wo)


def _norm_kernel(x_ref, g_ref, o_ref):
    x = x_ref[...]
    o_ref[...] = x * lax.rsqrt(jnp.mean(x * x, axis=-1, keepdims=True) + EPS) * g_ref[...]


def _final_norm(x, g, *, tm):
    t, d = x.shape
    row = pl.BlockSpec((tm, d), lambda i: (i, 0))
    return pl.pallas_call(
        _norm_kernel,
        grid=(t // tm,),
        in_specs=[row, pl.BlockSpec((1, d), lambda i: (0, 0))],
        out_specs=row,
        out_shape=jax.ShapeDtypeStruct((t, d), F32),
        compiler_params=_params("parallel"),
        name="final_norm",
    )(x, g)


def _tile(n, target):
    if n <= target:
        return n
    for c in range(target, 7, -1):
        if n % c == 0 and c % 8 == 0:
            return c
    return n


def _layer_weights(p, l, d, fc):
    f = p["w_ffn1_d"].shape[1]
    nc = f // fc

    def ffn(w_gu, w_d):
        wg = w_gu[l, :, :f].reshape(d, nc, fc).transpose(1, 0, 2).astype(BF16)
        wu = w_gu[l, :, f:].reshape(d, nc, fc).transpose(1, 0, 2).astype(BF16)
        return wg, wu, w_d[l].reshape(nc, fc, d).astype(BF16)

    ng = p["w_in"].shape[2] // d
    return dict(
        ffn1=ffn(p["w_ffn1_gu"], p["w_ffn1_d"]),
        ffn2=ffn(p["w_ffn2_gu"], p["w_ffn2_d"]),
        w_in=p["w_in"][l].reshape(d, ng, d).transpose(1, 0, 2).astype(BF16),
        w_ba=p["w_br_att"][l].astype(BF16),
        w_bh=p["w_br_hg"][l].astype(BF16),
        w_out=p["w_out"][l].astype(BF16),
    )


def _trunk(x3, mods, p, lb_all, weights, cache):
    nb, t, d = x3.shape
    depth = p["w_ada"].shape[0]
    nh = d // LANES
    bt = nb * t
    x = x3.reshape(bt, d)
    per_row = t < 256
    tm = _tile(bt if per_row else t, 512)
    kst = jnp.zeros((depth, bt * nh, LANES), F32)
    vst = jnp.zeros((depth, bt * nh, LANES), F32)
    states = []
    for l in range(depth):
        w = weights[l]
        m = mods[l]
        if per_row:
            mm = [jnp.repeat(m[:, i], t, axis=0) for i in range(N_ADA)]
        else:
            mm = [m[:, i].reshape(nb, 1, d) for i in range(N_ADA)]
        sh1, sc1, g1, sh2, sc2, g2, sh3, sc3, g3 = mm
        kw = dict(tm=tm, per_row=per_row, rows_per_seq=t)
        row = lambda a: a.reshape(1, -1)

        x = _ffn(x, row(p["g_ffn1"][l]), sh1, sc1, g1, *w["ffn1"], **kw)
        q_scale = (LANES // 2) ** -0.5 * math.log2(math.e)
        kw_in = dict(kw, tm=_tile(tm, 256))
        z, qkv, kst, vst = _inproj(x, row(p["g_mix"][l]), sh2, sc2, w["w_in"], kst, vst, l,
                                   q_scale=q_scale, **kw_in)

        lam_init = 0.8 - 0.6 * math.exp(-0.3 * l)
        cst = jnp.array([lam_init], F32)
        gsub = row(p["g_att_sub"][l])
        if cache is None:
            assert t % 16 == 0, t
            oa = _attn(cst, p["att_lambda"][l], gsub, qkv, l, nb=nb, t=t, nh=nh,
                       tq=_tile(t // 2, 512))
            s0 = None
        else:
            ck, cv, s0 = cache
            oa = _attn_cached(cst, p["att_lambda"][l], gsub, qkv, ck, cv, l, nb=nb, t=t, nh=nh)
        tt = _tile(t, 512)
        blk = min(tt, 128)
        oh, s_fin = _hgrn(z, row(lb_all[l]), row(p["g_hg_norm"][l]), s0, l,
                          nb=nb, t=t, nh=nh, tt=tt, blk=blk)
        states.append(s_fin)

        x = _merge(x, g2, oa, oh, z, w["w_ba"], w["w_bh"], w["w_out"], **kw)
        x = _ffn(x, row(p["g_ffn2"][l]), sh3, sc3, g3, *w["ffn2"], **kw)

    y = _final_norm(x, p["g_final"].reshape(1, d), tm=tm).reshape(nb, t, d)
    shape5 = (depth, nb, t, nh, LANES)
    return y, kst.reshape(shape5), vst.reshape(shape5), jnp.stack(states)


def kernel(x_prompt, x_sample, cache_k, cache_v, state_hgrn, c_prompt, c_sample, w_ada, b_ada, g_ffn1, w_ffn1_gu, w_ffn1_d, g_mix, w_in, att_lambda, g_att_sub, hg_lb_logits, g_hg_norm, w_br_att, w_br_hg, w_out, g_ffn2, w_ffn2_gu, w_ffn2_d, g_final):
    p = dict(w_ada=w_ada, b_ada=b_ada, g_ffn1=g_ffn1, w_ffn1_gu=w_ffn1_gu, w_ffn1_d=w_ffn1_d,
             g_mix=g_mix, w_in=w_in, att_lambda=att_lambda, g_att_sub=g_att_sub,
             g_hg_norm=g_hg_norm, w_br_att=w_br_att, w_br_hg=w_br_hg, w_out=w_out,
             g_ffn2=g_ffn2, w_ffn2_gu=w_ffn2_gu, w_ffn2_d=w_ffn2_d, g_final=g_final)
    depth, d, _ = w_ada.shape
    nbp, nbs = x_prompt.shape[0], x_sample.shape[0]

    lb_sm = jax.nn.softmax(hg_lb_logits.astype(F32), axis=0)
    lb_all = jnp.cumsum(lb_sm, axis=0) - lb_sm[0]

    ns = nbp + nbs
    nsp = -(-ns // 8) * 8
    c_all = jnp.concatenate([c_prompt, c_sample, jnp.zeros((nsp - ns, d), F32)], axis=0)
    mods = _ada(c_all, w_ada, b_ada).reshape(depth, nsp, N_ADA, d)

    f = w_ffn1_d.shape[1]
    fc = 256 if f % 256 == 0 else LANES
    weights = [_layer_weights(p, l, d, fc) for l in range(depth)]

    y_p, k_p, v_p, s_p = _trunk(x_prompt, mods[:, :nbp], p, lb_all, weights, None)
    y_s, k_s, v_s, s_s = _trunk(x_sample, mods[:, nbp:ns], p, lb_all, weights,
                                (cache_k, cache_v, state_hgrn))
    return (y_p, y_s, k_p, v_p, s_p, k_s, v_s, s_s)
```

```python
import functools
import math

import jax
import jax.numpy as jnp
from jax import lax
from jax.experimental import pallas as pl
from jax.experimental.pallas import tpu as pltpu

F32 = jnp.float32
BF16 = jnp.bfloat16

CHUNK = 64
N_ADA = 9
MACARON_WEIGHT = 0.5
EPS = 1e-6
MASK_VALUE = -1e30
TINY = 1e-30
LANES = 128
SUB = 8
VMEM_LIMIT = 56 * 1024 * 1024

_NT = (((1,), (1,)), ((), ()))
_TN = (((0,), (0,)), ((), ()))


def _params(*sem):
    return pltpu.CompilerParams(dimension_semantics=sem, vmem_limit_bytes=VMEM_LIMIT)


def _resident(shape):
    nd = len(shape)
    return pl.BlockSpec(shape, lambda *_: (0,) * nd, pipeline_mode=pl.Buffered(1))


def _modnorm(x, g, sc, sh):
    y = x * lax.rsqrt(jnp.mean(x * x, axis=-1, keepdims=True) + EPS) * g
    return y * (1.0 + sc) + sh


def _silu(x):
    return x * jax.nn.sigmoid(x)


def _mod_spec(per_row, tm, d, rows_per_seq):
    if per_row:
        return pl.BlockSpec((tm, d), lambda i, *_: (i, 0))
    tiles = rows_per_seq // tm
    return pl.BlockSpec((None, 1, d), lambda i, *_: (i // tiles, 0, 0))


def _ada_kernel(c_ref, w_ref, b_ref, o_ref):
    c = c_ref[...]
    o_ref[...] = jnp.dot(_silu(c).astype(BF16), w_ref[...].astype(BF16),
                         preferred_element_type=F32) + b_ref[...]


def _ada(c_all, w_ada, b_ada):
    depth, d, _ = w_ada.shape
    ns = c_all.shape[0]
    return pl.pallas_call(
        _ada_kernel,
        grid=(depth, N_ADA),
        in_specs=[pl.BlockSpec((ns, d), lambda l, j: (0, 0)),
                  pl.BlockSpec((None, d, d), lambda l, j: (l, 0, j)),
                  pl.BlockSpec((None, 1, d), lambda l, j: (l, 0, j))],
        out_specs=pl.BlockSpec((None, ns, d), lambda l, j: (l, 0, j)),
        out_shape=jax.ShapeDtypeStruct((depth, ns, N_ADA * d), F32),
        compiler_params=_params("parallel", "parallel"),
        name="ada",
    )(c_all, w_ada, b_ada.reshape(depth, 1, N_ADA * d))


def _ffn_kernel(x_ref, g_ref, sh_ref, sc_ref, gt_ref, wg_ref, wu_ref, wd_ref, o_ref,
                h_scr, acc_scr, *, n_chunks):
    h_scr[...] = _modnorm(x_ref[...], g_ref[...], sc_ref[...], sh_ref[...]).astype(BF16)
    for c in range(n_chunks):
        h = h_scr[...]
        gate = jnp.dot(h, wg_ref[c], preferred_element_type=F32)
        up = jnp.dot(h, wu_ref[c], preferred_element_type=F32)
        a = (_silu(gate) * up).astype(BF16)
        down = jnp.dot(a, wd_ref[c], preferred_element_type=F32)
        if c == 0:
            acc_scr[...] = down
        else:
            acc_scr[...] += down
    o_ref[...] = x_ref[...] + MACARON_WEIGHT * gt_ref[...] * acc_scr[...]


def _ffn(x, g, sh, sc, gt, wg, wu, wd, *, tm, per_row, rows_per_seq):
    t, d = x.shape
    n_chunks = wg.shape[0]
    mod = _mod_spec(per_row, tm, d, rows_per_seq)
    row = pl.BlockSpec((tm, d), lambda i: (i, 0))
    return pl.pallas_call(
        functools.partial(_ffn_kernel, n_chunks=n_chunks),
        grid=(t // tm,),
        in_specs=[row, pl.BlockSpec((1, d), lambda i: (0, 0)), mod, mod, mod,
                  _resident(wg.shape), _resident(wu.shape), _resident(wd.shape)],
        out_specs=row,
        out_shape=jax.ShapeDtypeStruct((t, d), F32),
        scratch_shapes=[pltpu.VMEM((tm, d), BF16), pltpu.VMEM((tm, d), F32)],
        compiler_params=_params("parallel"),
        name="ffn",
    )(x, g, sh, sc, gt, wg, wu, wd)


N_QKV = 3


def _inproj_kernel(x_ref, g_ref, sh_ref, sc_ref, w_ref, kin_ref, vin_ref,
                   z_ref, qkv_ref, k_ref, v_ref, u_scr, *, q_scale):
    del kin_ref, vin_ref
    j = pl.program_id(1)
    d = u_scr.shape[1]

    def group(g):
        return jnp.dot(u_scr[...], w_ref[g], preferred_element_type=F32)

    def cache_rows(ref, z):
        tm, nh = z.shape[0], z.shape[1] // LANES
        for h in range(nh):
            ref[pl.ds(h, tm, stride=nh), :] = z[:, h * LANES:(h + 1) * LANES]

    @pl.when(j == 0)
    def _():
        u_scr[...] = _modnorm(x_ref[...], g_ref[...], sc_ref[...], sh_ref[...]).astype(BF16)
        qkv_ref[:, 0:d] = (group(0) * q_scale).astype(BF16)
        for g, ref in ((1, k_ref), (2, v_ref)):
            z = group(g)
            cache_rows(ref, z)
            qkv_ref[:, g * d:(g + 1) * d] = z.astype(BF16)

    @pl.when(j > 0)
    def _():
        for g in range(N_QKV):
            z_ref[:, g * d:(g + 1) * d] = group(N_QKV * j + g)


def _inproj(x, g, sh, sc, w3, kst, vst, layer, *, tm, per_row, rows_per_seq, q_scale):
    t, d = x.shape
    ng = w3.shape[0]
    mod = _mod_spec(per_row, tm, d, rows_per_seq)
    row = pl.BlockSpec((tm, d), lambda i, j: (i, 0))
    stack = pl.BlockSpec((None, tm * (d // LANES), LANES), lambda i, j: (layer, i, 0))
    anyspec = pl.BlockSpec(memory_space=pl.ANY)
    return pl.pallas_call(
        functools.partial(_inproj_kernel, q_scale=q_scale),
        grid=(t // tm, ng // N_QKV),
        in_specs=[row, pl.BlockSpec((1, d), lambda i, j: (0, 0)), mod, mod,
                  _resident(w3.shape), anyspec, anyspec],
        out_specs=[pl.BlockSpec((tm, N_QKV * d), lambda i, j: (i, jnp.maximum(j - 1, 0))),
                   pl.BlockSpec((tm, N_QKV * d), lambda i, j: (i, 0)),
                   stack, stack],
        out_shape=[jax.ShapeDtypeStruct((t, (ng - N_QKV) * d), F32),
                   jax.ShapeDtypeStruct((t, N_QKV * d), BF16),
                   jax.ShapeDtypeStruct(kst.shape, F32),
                   jax.ShapeDtypeStruct(vst.shape, F32)],
        scratch_shapes=[pltpu.VMEM((tm, d), BF16)],
        input_output_aliases={5: 2, 6: 3},
        compiler_params=_params("parallel", "arbitrary"),
        name="inproj",
    )(x, g, sh, sc, w3, kst, vst)


def _lambda(lamp_ref, lam_init):
    lp = lamp_ref[...]
    s01 = jnp.sum(lp[0:1] * lp[1:2], axis=-1, keepdims=True)
    s23 = jnp.sum(lp[2:3] * lp[3:4], axis=-1, keepdims=True)
    return jnp.exp(s01) - jnp.exp(s23) + lam_init


def _split_maps(q, hd):
    lane = lax.broadcasted_iota(jnp.int32, q.shape, 1)
    zero = jnp.zeros_like(q)
    return jnp.concatenate([jnp.where(lane < hd, q, zero), jnp.where(lane >= hd, q, zero)],
                           axis=0).astype(BF16)


def _visible(rows, cols, tq, q0, k0):
    r = lax.broadcasted_iota(jnp.int32, (rows, cols), 0)
    qpos = q0 + jnp.where(r >= tq, r - tq, r)
    kpos = k0 + lax.broadcasted_iota(jnp.int32, (rows, cols), 1)
    return kpos < (qpos // CHUNK + 1) * CHUNK


def _attn_finish(o1, o2, lam, lam_init, gsub):
    o = o1 - lam * o2
    y = o * lax.rsqrt(jnp.mean(o * o, axis=-1, keepdims=True) + EPS) * gsub
    return y * (1.0 - lam_init)


def _attn_kernel(cst_ref, lamp_ref, gsub_ref, q_ref, k_ref, v_ref, o_ref,
                 qs_scr, vx_scr, m_scr, acc_scr, *, tq, hd):
    i = pl.program_id(2)
    lam_init = cst_ref[0]

    @pl.when(i == 0)
    def _():
        vx_scr[:, :LANES] = v_ref[...]
        vx_scr[:, LANES:] = jnp.ones((vx_scr.shape[0], LANES), BF16)

    q = q_ref[...]
    lane = lax.broadcasted_iota(jnp.int32, q.shape, 1)
    qs_scr[:tq] = jnp.where(lane < hd, q, jnp.zeros_like(q))
    qs_scr[tq:] = jnp.where(lane >= hd, q, jnp.zeros_like(q))
    m_scr[...] = jnp.full_like(m_scr, MASK_VALUE)
    acc_scr[...] = jnp.zeros_like(acc_scr)

    def kv_step(start, width, masked):
        k = k_ref[pl.ds(start, width), :]
        vx = vx_scr[pl.ds(start, width), :]
        if masked:
            vis = _visible(tq, width, tq, i * tq, start)
        maps = (slice(0, tq), slice(tq, 2 * tq))
        scores = [lax.dot_general(qs_scr[rows], k, _NT, preferred_element_type=F32)
                  for rows in maps]
        for rows, s in zip(maps, scores):
            if masked:
                s = jnp.where(vis, s, MASK_VALUE)
            m_old = m_scr[rows]
            m_new = jnp.maximum(m_old, jnp.max(s, axis=-1, keepdims=True))
            p = jnp.exp2(s - jnp.tile(m_new, (1, width // LANES))).astype(BF16)
            alpha = jnp.exp2(m_old - m_new)
            acc_scr[rows] = (jnp.tile(alpha, (1, 2)) * acc_scr[rows]
                             + jnp.dot(p, vx, preferred_element_type=F32))
            m_scr[rows] = m_new

    def full_body(j, carry):
        kv_step(pl.multiple_of(j * 2 * tq, 2 * tq), 2 * tq, False)
        return carry

    lax.fori_loop(0, i // 2, full_body, 0)

    @pl.when(i % 2 == 1)
    def _():
        kv_step(pl.multiple_of((i - 1) * tq, tq), tq, False)

    kv_step(pl.multiple_of(i * tq, tq), tq, True)

    acc = acc_scr[...]
    o12 = acc[:, :LANES] / acc[:, LANES:]
    lam = _lambda(lamp_ref, lam_init)
    o_ref[...] = _attn_finish(o12[:tq], o12[tq:], lam, lam_init, gsub_ref[...])


def _attn(cst, lamp, gsub, qkv, layer, *, nb, t, nh, tq):
    del layer
    bt, d = qkv.shape[0], nh * LANES
    nq = t // tq

    def kv(group):
        return pl.BlockSpec((t, LANES), lambda b, h, i: (b, group * nh + h))

    return pl.pallas_call(
        functools.partial(_attn_kernel, tq=tq, hd=LANES // 2),
        grid=(nb, nh, nq),
        in_specs=[pl.BlockSpec(memory_space=pltpu.SMEM),
                  pl.BlockSpec(lamp.shape, lambda b, h, i: (0, 0)),
                  pl.BlockSpec((1, LANES), lambda b, h, i: (0, 0)),
                  pl.BlockSpec((tq, LANES), lambda b, h, i: (b * nq + i, h)),
                  kv(1), kv(2)],
        out_specs=pl.BlockSpec((tq, LANES), lambda b, h, i: (b * nq + i, h)),
        out_shape=jax.ShapeDtypeStruct((bt, d), F32),
        scratch_shapes=[pltpu.VMEM((2 * tq, LANES), BF16), pltpu.VMEM((t, 2 * LANES), BF16),
                        pltpu.VMEM((2 * tq, LANES), F32), pltpu.VMEM((2 * tq, 2 * LANES), F32)],
        compiler_params=_params("parallel", "parallel", "arbitrary"),
        name="attn",
    )(cst, lamp, gsub, qkv, qkv, qkv)


def _attn_cached_kernel(cst_ref, lamp_ref, gsub_ref, q_ref, kc_ref, vc_ref, kn_ref, vn_ref, o_ref,
                        *, tq, past, hd):
    lam_init = cst_ref[0]
    lam = _lambda(lamp_ref, lam_init)
    nh = q_ref.shape[1] // LANES
    masked = past % CHUNK + tq > CHUNK
    for h in range(nh):
        cols = slice(h * LANES, (h + 1) * LANES)
        head = pl.ds(h, past, stride=nh)
        qs = _split_maps(q_ref[:, cols], hd)

        def scores(k, k0):
            s = lax.dot_general(qs, k.astype(BF16), _NT, preferred_element_type=F32)
            if masked:
                s = jnp.where(_visible(2 * tq, k.shape[0], tq, past, k0), s, MASK_VALUE)
            return s

        sc = scores(kc_ref[head, :], 0)
        sn = scores(kn_ref[:, cols], past)
        m = jnp.maximum(jnp.max(sc, axis=-1, keepdims=True), jnp.max(sn, axis=-1, keepdims=True))
        pc = jnp.exp2(sc - m)
        pn = jnp.exp2(sn - m)
        l = jnp.sum(pc, axis=-1, keepdims=True) + jnp.sum(pn, axis=-1, keepdims=True)
        acc = (jnp.dot(pc.astype(BF16), vc_ref[head, :].astype(BF16), preferred_element_type=F32)
               + jnp.dot(pn.astype(BF16), vn_ref[:, cols], preferred_element_type=F32))
        o12 = acc / l
        o_ref[:, cols] = _attn_finish(o12[:tq], o12[tq:], lam, lam_init, gsub_ref[...])


def _attn_cached(cst, lamp, gsub, qkv, ck, cv, layer, *, nb, t, nh):
    bt, d = qkv.shape[0], nh * LANES
    past = ck.shape[2]
    ck, cv = (c.reshape(c.shape[0], nb, past * nh, LANES) for c in (ck, cv))
    cache = pl.BlockSpec((None, None, past * nh, LANES), lambda b: (layer, b, 0, 0))

    def rows(group):
        return pl.BlockSpec((t, d), lambda b: (b, group))

    return pl.pallas_call(
        functools.partial(_attn_cached_kernel, tq=t, past=past, hd=LANES // 2),
        grid=(nb,),
        in_specs=[pl.BlockSpec(memory_space=pltpu.SMEM),
                  pl.BlockSpec(lamp.shape, lambda b: (0, 0)),
                  pl.BlockSpec((1, LANES), lambda b: (0, 0)),
                  rows(0), cache, cache, rows(1), rows(2)],
        out_specs=pl.BlockSpec((t, d), lambda b: (b, 0)),
        out_shape=jax.ShapeDtypeStruct((bt, d), F32),
        compiler_params=_params("parallel"),
        name="attn_cached",
    )(cst, lamp, gsub, qkv, ck, cv, qkv, qkv)


def _split3(x):
    hi = x.astype(BF16)
    r = x - hi.astype(F32)
    mid = r.astype(BF16)
    lo = (r - mid.astype(F32)).astype(BF16)
    return hi, mid, lo


def _hgrn_kernel(*refs, tt, blk, has_s0):
    hq_ref, hf_ref, hi_ref, hg_ref, lb_ref, gn_ref, tri_ref, ones2_ref, lvl_ref = refs[:9]
    s0_ref = refs[9] if has_s0 else None
    o_ref, sf_ref, st_scr, pad_scr = refs[-4:]
    t = pl.program_id(2)

    @pl.when(t == 0)
    def _():
        if has_s0:
            st_scr[...] = s0_ref[...].T
        else:
            st_scr[...] = jnp.zeros_like(st_scr)
        pad_scr[:, 0:SUB, :] = jnp.zeros((3, SUB, LANES), F32)

    lb = lb_ref[...]
    sig = jax.nn.sigmoid(hf_ref[...])
    logf = jnp.log(jnp.maximum(lb + (1.0 - lb) * sig, TINY)) * math.log2(math.e)
    kk = (1.0 - lb) * (1.0 - sig)
    qq = _silu(hq_ref[...])
    vv = hi_ref[...]

    tri = tri_ref[...]
    parts = _split3(logf)
    b = jnp.concatenate(
        [sum(jnp.dot(tri, part[c * blk:(c + 1) * blk], preferred_element_type=F32)
             for part in parts) for c in range(tt // blk)], axis=0)

    pad_scr[0, SUB:, :] = kk
    pad_scr[1, SUB:, :] = b
    pad_scr[2, SUB:, :] = vv
    ones2 = ones2_ref[...]
    pos = lax.broadcasted_iota(jnp.int32, (SUB, LANES), 0)

    def lag_terms(d):
        if d == 0:
            return (qq * kk).astype(BF16), vv
        ks, bs, vs = (pad_scr[r, SUB - d:SUB - d + tt, :] for r in range(3))
        off = jnp.tile(jnp.where(pos >= d, 0.0, MASK_VALUE), (tt // SUB, 1))
        return (qq * ks * jnp.exp2(b - bs + off)).astype(BF16), vs

    o = jnp.zeros((tt, LANES), F32)
    for d in range(0, SUB, 2):
        (p0, v0), (p1, v1) = lag_terms(d), lag_terms(d + 1)
        r = jnp.dot(jnp.concatenate([p0, p1], axis=1), ones2, preferred_element_type=F32)
        o = o + r[:, :LANES] * v0 + r[:, LANES:] * v1

    st = st_scr[...]
    older = []
    for c in range(tt // blk):
        sl = slice(c * blk, (c + 1) * blk)
        bc = b[sl]
        b_last = bc[blk - 1:blk]
        qt = (qq[sl] * jnp.exp2(bc)).astype(BF16)
        older.append(lax.dot_general(qt, st.astype(BF16), _NT, preferred_element_type=F32))
        kt = (kk[sl] * jnp.exp2(b_last - bc)).astype(BF16)
        st = st * jnp.exp2(b_last) + lax.dot_general(vv[sl].astype(BF16), kt, _TN,
                                                    preferred_element_type=F32)
    st_scr[...] = st
    o = o + jnp.concatenate(older, axis=0)

    weights = [None] * (tt // blk)
    m, level = SUB, 0
    while 2 * m <= blk:
        b3 = b.reshape(tt // (2 * m), 2 * m, LANES)
        w = jnp.exp2(-jnp.abs(b3 - b3[:, m - 1:m, :])).reshape(tt, LANES)
        qt = (qq * w).astype(BF16)
        kt = (kk * w).astype(BF16)
        for c in range(tt // blk):
            sl = slice(c * blk, (c + 1) * blk)
            a = lax.dot_general(qt[sl], kt[sl], _NT, preferred_element_type=F32) * lvl_ref[level]
            weights[c] = a if weights[c] is None else weights[c] + a
        m, level = 2 * m, level + 1
    if level:
        o = o + jnp.concatenate(
            [jnp.dot(weights[c].astype(BF16), vv[c * blk:(c + 1) * blk].astype(BF16),
                     preferred_element_type=F32) for c in range(tt // blk)], axis=0)

    y = o * lax.rsqrt(jnp.mean(o * o, axis=-1, keepdims=True) + EPS) * gn_ref[...]
    o_ref[...] = y * _silu(hg_ref[...])

    @pl.when(t == pl.num_programs(2) - 1)
    def _():
        sf_ref[...] = st.T


def _hgrn(z, lb, gn, s0, layer, *, nb, t, nh, tt, blk):
    bt = z.shape[0]
    nt = t // tt
    has_s0 = s0 is not None

    def col(group):
        return pl.BlockSpec((tt, LANES), lambda b, h, i: (b * nt + i, group * nh + h))

    state = pl.BlockSpec((None, None, LANES, LANES), lambda b, h, i: (b, h, 0, 0))
    tri = jnp.tril(jnp.ones((blk, blk), BF16))
    ones2 = jnp.kron(jnp.eye(2, dtype=BF16), jnp.ones((LANES, LANES), BF16))
    pos = jnp.arange(blk)
    sizes = [SUB << i for i in range(max(blk // SUB, 1).bit_length() - 1)]
    lvl = jnp.stack([((pos[:, None] // (2 * m) == pos[None, :] // (2 * m))
                      & (pos[:, None] % (2 * m) >= m) & (pos[None, :] % (2 * m) < m))
                     for m in sizes]).astype(F32) if sizes else jnp.zeros((1, blk, blk), F32)
    in_specs = [col(0), col(1), col(2), col(3),
                pl.BlockSpec((1, LANES), lambda b, h, i: (0, h)),
                pl.BlockSpec((1, LANES), lambda b, h, i: (0, 0)),
                _resident(tri.shape), _resident(ones2.shape), _resident(lvl.shape)]
    args = [z, z, z, z, lb, gn, tri, ones2, lvl]
    if has_s0:
        in_specs.append(pl.BlockSpec((None, None, None, LANES, LANES),
                                     lambda b, h, i: (layer, b, h, 0, 0)))
        args.append(s0)
    return pl.pallas_call(
        functools.partial(_hgrn_kernel, tt=tt, blk=blk, has_s0=has_s0),
        grid=(nb, nh, nt),
        in_specs=in_specs,
        out_specs=[pl.BlockSpec((tt, LANES), lambda b, h, i: (b * nt + i, h)), state],
        out_shape=[jax.ShapeDtypeStruct((bt, nh * LANES), F32),
                   jax.ShapeDtypeStruct((nb, nh, LANES, LANES), F32)],
        scratch_shapes=[pltpu.VMEM((LANES, LANES), F32), pltpu.VMEM((3, SUB + tt, LANES), F32)],
        compiler_params=_params("parallel", "parallel", "arbitrary"),
        name="hgrn",
    )(*args)


def _merge_kernel(x_ref, gt_ref, oa_ref, oh_ref, za_ref, zh_ref, wa_ref, wh_ref, wo_ref, o_ref):
    a = jnp.dot(oa_ref[...].astype(BF16), wa_ref[...], preferred_element_type=F32)
    r = jnp.dot(oh_ref[...].astype(BF16), wh_ref[...], preferred_element_type=F32)
    merged = jax.nn.sigmoid(za_ref[...]) * a + jax.nn.sigmoid(zh_ref[...]) * r
    o_ref[...] = x_ref[...] + gt_ref[...] * jnp.dot(merged.astype(BF16), wo_ref[...],
                                                    preferred_element_type=F32)


def _merge(x, gt, oa, oh, z, wa, wh, wo, *, tm, per_row, rows_per_seq):
    t, d = x.shape
    mod = _mod_spec(per_row, tm, d, rows_per_seq)
    row = pl.BlockSpec((tm, d), lambda i: (i, 0))
    return pl.pallas_call(
        _merge_kernel,
        grid=(t // tm,),
        in_specs=[row, mod, row, row,
                  pl.BlockSpec((tm, d), lambda i: (i, 4)), pl.BlockSpec((tm, d), lambda i: (i, 5)),
                  _resident(wa.shape), _resident(wh.shape), _resident(wo.shape)],
        out_specs=row,
        out_shape=jax.ShapeDtypeStruct((t, d), F32),
        compiler_params=_params("parallel"),
        name="merge",
    )(x, gt, oa, oh, z, z, wa, wh, wo)


def _norm_kernel(x_ref, g_ref, o_ref):
    x = x_ref[...]
    o_ref[...] = x * lax.rsqrt(jnp.mean(x * x, axis=-1, keepdims=True) + EPS) * g_ref[...]


def _final_norm(x, g, *, tm):
    t, d = x.shape
    row = pl.BlockSpec((tm, d), lambda i: (i, 0))
    return pl.pallas_call(
        _norm_kernel,
        grid=(t // tm,),
        in_specs=[row, pl.BlockSpec((1, d), lambda i: (0, 0))],
        out_specs=row,
        out_shape=jax.ShapeDtypeStruct((t, d), F32),
        compiler_params=_params("parallel"),
        name="final_norm",
    )(x, g)


def _tile(n, target):
    if n <= target:
        return n
    for c in range(target, 7, -1):
        if n % c == 0 and c % 8 == 0:
            return c
    return n


def _layer_weights(p, l, d, fc):
    f = p["w_ffn1_d"].shape[1]
    nc = f // fc

    def ffn(w_gu, w_d):
        wg = w_gu[l, :, :f].reshape(d, nc, fc).transpose(1, 0, 2).astype(BF16)
        wu = w_gu[l, :, f:].reshape(d, nc, fc).transpose(1, 0, 2).astype(BF16)
        return wg, wu, w_d[l].reshape(nc, fc, d).astype(BF16)

    ng = p["w_in"].shape[2] // d
    return dict(
        ffn1=ffn(p["w_ffn1_gu"], p["w_ffn1_d"]),
        ffn2=ffn(p["w_ffn2_gu"], p["w_ffn2_d"]),
        w_in=p["w_in"][l].reshape(d, ng, d).transpose(1, 0, 2).astype(BF16),
        w_ba=p["w_br_att"][l].astype(BF16),
        w_bh=p["w_br_hg"][l].astype(BF16),
        w_out=p["w_out"][l].astype(BF16),
    )


def _trunk(x3, mods, p, lb_all, weights, cache):
    nb, t, d = x3.shape
    depth = p["w_ada"].shape[0]
    nh = d // LANES
    bt = nb * t
    x = x3.reshape(bt, d)
    per_row = t < 256
    tm = _tile(bt if per_row else t, 512)
    kst = jnp.zeros((depth, bt * nh, LANES), F32)
    vst = jnp.zeros((depth, bt * nh, LANES), F32)
    states = []
    for l in range(depth):
        w = weights[l]
        m = mods[l]
        if per_row:
            mm = [jnp.repeat(m[:, i], t, axis=0) for i in range(N_ADA)]
        else:
            mm = [m[:, i].reshape(nb, 1, d) for i in range(N_ADA)]
        sh1, sc1, g1, sh2, sc2, g2, sh3, sc3, g3 = mm
        kw = dict(tm=tm, per_row=per_row, rows_per_seq=t)
        row = lambda a: a.reshape(1, -1)

        x = _ffn(x, row(p["g_ffn1"][l]), sh1, sc1, g1, *w["ffn1"], **kw)
        q_scale = (LANES // 2) ** -0.5 * math.log2(math.e)
        kw_in = dict(kw, tm=_tile(tm, 256))
        z, qkv, kst, vst = _inproj(x, row(p["g_mix"][l]), sh2, sc2, w["w_in"], kst, vst, l,
                                   q_scale=q_scale, **kw_in)

        lam_init = 0.8 - 0.6 * math.exp(-0.3 * l)
        cst = jnp.array([lam_init], F32)
        gsub = row(p["g_att_sub"][l])
        if cache is None:
            oa = _attn(cst, p["att_lambda"][l], gsub, qkv, l, nb=nb, t=t, nh=nh, tq=_tile(t, 512))
            s0 = None
        else:
            ck, cv, s0 = cache
            oa = _attn_cached(cst, p["att_lambda"][l], gsub, qkv, ck, cv, l, nb=nb, t=t, nh=nh)
        tt = _tile(t, 2048)
        blk = min(tt, 128)
        oh, s_fin = _hgrn(z, row(lb_all[l]), row(p["g_hg_norm"][l]), s0, l,
                          nb=nb, t=t, nh=nh, tt=tt, blk=blk)
        states.append(s_fin)

        x = _merge(x, g2, oa, oh, z, w["w_ba"], w["w_bh"], w["w_out"], **kw)
        x = _ffn(x, row(p["g_ffn2"][l]), sh3, sc3, g3, *w["ffn2"], **kw)

    y = _final_norm(x, p["g_final"].reshape(1, d), tm=tm).reshape(nb, t, d)
    shape5 = (depth, nb, t, nh, LANES)
    return y, kst.reshape(shape5), vst.reshape(shape5), jnp.stack(states)


def kernel(x_prompt, x_sample, cache_k, cache_v, state_hgrn, c_prompt, c_sample, w_ada, b_ada, g_ffn1, w_ffn1_gu, w_ffn1_d, g_mix, w_in, att_lambda, g_att_sub, hg_lb_logits, g_hg_norm, w_br_att, w_br_hg, w_out, g_ffn2, w_ffn2_gu, w_ffn2_d, g_final):
    p = dict(w_ada=w_ada, b_ada=b_ada, g_ffn1=g_ffn1, w_ffn1_gu=w_ffn1_gu, w_ffn1_d=w_ffn1_d,
             g_mix=g_mix, w_in=w_in, att_lambda=att_lambda, g_att_sub=g_att_sub,
             g_hg_norm=g_hg_norm, w_br_att=w_br_att, w_br_hg=w_br_hg, w_out=w_out,
             g_ffn2=g_ffn2, w_ffn2_gu=w_ffn2_gu, w_ffn2_d=w_ffn2_d, g_final=g_final)
    depth, d, _ = w_ada.shape
    nbp, nbs = x_prompt.shape[0], x_sample.shape[0]

    lb_sm = jax.nn.softmax(hg_lb_logits.astype(F32), axis=0)
    lb_all = jnp.cumsum(lb_sm, axis=0) - lb_sm[0]

    ns = nbp + nbs
    nsp = -(-ns // 8) * 8
    c_all = jnp.concatenate([c_prompt, c_sample, jnp.zeros((nsp - ns, d), F32)], axis=0)
    mods = _ada(c_all, w_ada, b_ada).reshape(depth, nsp, N_ADA, d)

    f = w_ffn1_d.shape[1]
    fc = 256 if f % 256 == 0 else LANES
    weights = [_layer_weights(p, l, d, fc) for l in range(depth)]

    y_p, k_p, v_p, s_p = _trunk(x_prompt, mods[:, :nbp], p, lb_all, weights, None)
    y_s, k_s, v_s, s_s = _trunk(x_sample, mods[:, nbp:ns], p, lb_all, weights,
                                (cache_k, cache_v, state_hgrn))
    return (y_p, y_s, k_p, v_p, s_p, k_s, v_s, s_s)
```

```python
import functools
import math

import jax
import jax.numpy as jnp
from jax import lax
from jax.experimental import pallas as pl
from jax.experimental.pallas import tpu as pltpu

F32 = jnp.float32
BF16 = jnp.bfloat16

CHUNK = 64
N_ADA = 9
MACARON_WEIGHT = 0.5
EPS = 1e-6
MASK_VALUE = -1e30
TINY = 1e-30
LANES = 128
SUB = 8
VMEM_LIMIT = 56 * 1024 * 1024

_NT = (((1,), (1,)), ((), ()))
_TN = (((0,), (0,)), ((), ()))


def _params(*sem):
    return pltpu.CompilerParams(dimension_semantics=sem, vmem_limit_bytes=VMEM_LIMIT)


def _resident(shape):
    nd = len(shape)
    return pl.BlockSpec(shape, lambda *_: (0,) * nd, pipeline_mode=pl.Buffered(1))


def _modnorm(x, g, sc, sh):
    y = x * lax.rsqrt(jnp.mean(x * x, axis=-1, keepdims=True) + EPS) * g
    return y * (1.0 + sc) + sh


def _silu(x):
    return x * jax.nn.sigmoid(x)


def _mod_spec(per_row, tm, d, rows_per_seq):
    if per_row:
        return pl.BlockSpec((tm, d), lambda i, *_: (i, 0))
    tiles = rows_per_seq // tm
    return pl.BlockSpec((None, 1, d), lambda i, *_: (i // tiles, 0, 0))


def _ada_kernel(c_ref, w_ref, b_ref, o_ref):
    c = c_ref[...]
    o_ref[...] = jnp.dot(_silu(c).astype(BF16), w_ref[...].astype(BF16),
                         preferred_element_type=F32) + b_ref[...]


def _ada(c_all, w_ada, b_ada):
    depth, d, _ = w_ada.shape
    ns = c_all.shape[0]
    return pl.pallas_call(
        _ada_kernel,
        grid=(depth, N_ADA),
        in_specs=[pl.BlockSpec((ns, d), lambda l, j: (0, 0)),
                  pl.BlockSpec((None, d, d), lambda l, j: (l, 0, j)),
                  pl.BlockSpec((None, 1, d), lambda l, j: (l, 0, j))],
        out_specs=pl.BlockSpec((None, ns, d), lambda l, j: (l, 0, j)),
        out_shape=jax.ShapeDtypeStruct((depth, ns, N_ADA * d), F32),
        compiler_params=_params("parallel", "parallel"),
        name="ada",
    )(c_all, w_ada, b_ada.reshape(depth, 1, N_ADA * d))


def _ffn_kernel(x_ref, g_ref, sh_ref, sc_ref, gt_ref, wg_ref, wu_ref, wd_ref, gf_ref, o_ref,
                h_scr, acc_scr, *, n_chunks, final_norm):
    h_scr[...] = _modnorm(x_ref[...], g_ref[...], sc_ref[...], sh_ref[...]).astype(BF16)
    for c in range(n_chunks):
        h = h_scr[...]
        gate = jnp.dot(h, wg_ref[c], preferred_element_type=F32)
        up = jnp.dot(h, wu_ref[c], preferred_element_type=F32)
        a = (_silu(gate) * up).astype(BF16)
        down = jnp.dot(a, wd_ref[c], preferred_element_type=F32)
        if c == 0:
            acc_scr[...] = down
        else:
            acc_scr[...] += down
    y = x_ref[...] + MACARON_WEIGHT * gt_ref[...] * acc_scr[...]
    if final_norm:
        y = y * lax.rsqrt(jnp.mean(y * y, axis=-1, keepdims=True) + EPS) * gf_ref[...]
    o_ref[...] = y


def _ffn(x, g, sh, sc, gt, wg, wu, wd, g_final, *, tm, per_row, rows_per_seq, final_norm):
    t, d = x.shape
    n_chunks = wg.shape[0]
    mod = _mod_spec(per_row, tm, d, rows_per_seq)
    row = pl.BlockSpec((tm, d), lambda i: (i, 0))
    gain = pl.BlockSpec((1, d), lambda i: (0, 0))
    return pl.pallas_call(
        functools.partial(_ffn_kernel, n_chunks=n_chunks, final_norm=final_norm),
        grid=(t // tm,),
        in_specs=[row, gain, mod, mod, mod,
                  _resident(wg.shape), _resident(wu.shape), _resident(wd.shape), gain],
        out_specs=row,
        out_shape=jax.ShapeDtypeStruct((t, d), F32),
        scratch_shapes=[pltpu.VMEM((tm, d), BF16), pltpu.VMEM((tm, d), F32)],
        compiler_params=_params("parallel"),
        name="ffn",
    )(x, g, sh, sc, gt, wg, wu, wd, g_final)


N_QKV = 3


def _inproj_kernel(x_ref, g_ref, sh_ref, sc_ref, w_ref, kin_ref, vin_ref,
                   z_ref, qkv_ref, k_ref, v_ref, u_scr, *, q_scale):
    del kin_ref, vin_ref
    j = pl.program_id(1)
    d = u_scr.shape[1]

    def group(g):
        return jnp.dot(u_scr[...], w_ref[g], preferred_element_type=F32)

    def cache_rows(ref, z):
        tm, nh = z.shape[0], z.shape[1] // LANES
        for h in range(nh):
            ref[pl.ds(h, tm, stride=nh), :] = z[:, h * LANES:(h + 1) * LANES]

    @pl.when(j == 0)
    def _():
        u_scr[...] = _modnorm(x_ref[...], g_ref[...], sc_ref[...], sh_ref[...]).astype(BF16)
        qkv_ref[:, 0:d] = (group(0) * q_scale).astype(BF16)
        for g, ref in ((1, k_ref), (2, v_ref)):
            z = group(g)
            cache_rows(ref, z)
            qkv_ref[:, g * d:(g + 1) * d] = z.astype(BF16)

    @pl.when(j > 0)
    def _():
        for g in range(N_QKV):
            z_ref[:, g * d:(g + 1) * d] = group(N_QKV * j + g)


def _inproj(x, g, sh, sc, w3, kst, vst, layer, *, tm, per_row, rows_per_seq, q_scale):
    t, d = x.shape
    ng = w3.shape[0]
    mod = _mod_spec(per_row, tm, d, rows_per_seq)
    row = pl.BlockSpec((tm, d), lambda i, j: (i, 0))
    stack = pl.BlockSpec((None, tm * (d // LANES), LANES), lambda i, j: (layer, i, 0))
    anyspec = pl.BlockSpec(memory_space=pl.ANY)
    return pl.pallas_call(
        functools.partial(_inproj_kernel, q_scale=q_scale),
        grid=(t // tm, ng // N_QKV),
        in_specs=[row, pl.BlockSpec((1, d), lambda i, j: (0, 0)), mod, mod,
                  _resident(w3.shape), anyspec, anyspec],
        out_specs=[pl.BlockSpec((tm, N_QKV * d), lambda i, j: (i, jnp.maximum(j - 1, 0))),
                   pl.BlockSpec((tm, N_QKV * d), lambda i, j: (i, 0)),
                   stack, stack],
        out_shape=[jax.ShapeDtypeStruct((t, (ng - N_QKV) * d), F32),
                   jax.ShapeDtypeStruct((t, N_QKV * d), BF16),
                   jax.ShapeDtypeStruct(kst.shape, F32),
                   jax.ShapeDtypeStruct(vst.shape, F32)],
        scratch_shapes=[pltpu.VMEM((tm, d), BF16)],
        input_output_aliases={5: 2, 6: 3},
        compiler_params=_params("parallel", "arbitrary"),
        name="inproj",
    )(x, g, sh, sc, w3, kst, vst)


def _lambda(lamp_ref, lam_init):
    lp = lamp_ref[...]
    s01 = jnp.sum(lp[0:1] * lp[1:2], axis=-1, keepdims=True)
    s23 = jnp.sum(lp[2:3] * lp[3:4], axis=-1, keepdims=True)
    return jnp.exp(s01) - jnp.exp(s23) + lam_init


def _split_maps(q, hd):
    lane = lax.broadcasted_iota(jnp.int32, q.shape, 1)
    zero = jnp.zeros_like(q)
    return jnp.concatenate([jnp.where(lane < hd, q, zero), jnp.where(lane >= hd, q, zero)],
                           axis=0).astype(BF16)


def _visible(rows, cols, tq, q0, k0):
    r = lax.broadcasted_iota(jnp.int32, (rows, cols), 0)
    qpos = q0 + jnp.where(r >= tq, r - tq, r)
    kpos = k0 + lax.broadcasted_iota(jnp.int32, (rows, cols), 1)
    return kpos < (qpos // CHUNK + 1) * CHUNK


def _attn_finish(o1, o2, lam, lam_init, gsub):
    o = o1 - lam * o2
    y = o * lax.rsqrt(jnp.mean(o * o, axis=-1, keepdims=True) + EPS) * gsub
    return y * (1.0 - lam_init)


def _attn_kernel(cst_ref, lamp_ref, gsub_ref, q_ref, k_ref, v_ref, o_ref,
                 qs_scr, vx_scr, m_scr, acc_scr, *, tq, hd):
    i = pl.program_id(2)
    lam_init = cst_ref[0]

    @pl.when(i == 0)
    def _():
        vx_scr[:, :LANES] = v_ref[...]
        vx_scr[:, LANES:] = jnp.ones((vx_scr.shape[0], LANES), BF16)

    q = q_ref[...]
    lane = lax.broadcasted_iota(jnp.int32, q.shape, 1)
    qs_scr[:tq] = jnp.where(lane < hd, q, jnp.zeros_like(q))
    qs_scr[tq:] = jnp.where(lane >= hd, q, jnp.zeros_like(q))
    m_scr[...] = jnp.full_like(m_scr, MASK_VALUE)
    acc_scr[...] = jnp.zeros_like(acc_scr)

    def kv_step(start, width, masked):
        k = k_ref[pl.ds(start, width), :]
        vx = vx_scr[pl.ds(start, width), :]
        if masked:
            vis = _visible(tq, width, tq, i * tq, start)
        maps = (slice(0, tq), slice(tq, 2 * tq))
        scores = [lax.dot_general(qs_scr[rows], k, _NT, preferred_element_type=F32)
                  for rows in maps]
        for rows, s in zip(maps, scores):
            if masked:
                s = jnp.where(vis, s, MASK_VALUE)
            m_old = m_scr[rows]
            m_new = jnp.maximum(m_old, jnp.max(s, axis=-1, keepdims=True))
            p = jnp.exp2(s - jnp.tile(m_new, (1, width // LANES))).astype(BF16)
            alpha = jnp.exp2(m_old - m_new)
            acc_scr[rows] = (jnp.tile(alpha, (1, 2)) * acc_scr[rows]
                             + jnp.dot(p, vx, preferred_element_type=F32))
            m_scr[rows] = m_new

    def full_step(j):
        kv_step(pl.multiple_of(j * 2 * tq, 2 * tq), 2 * tq, False)

    def full_body(k, carry):
        full_step(2 * k)
        full_step(2 * k + 1)
        return carry

    n_full = i // 2
    lax.fori_loop(0, n_full // 2, full_body, 0)

    @pl.when(n_full % 2 == 1)
    def _():
        full_step(n_full - 1)

    @pl.when(i % 2 == 1)
    def _():
        kv_step(pl.multiple_of((i - 1) * tq, tq), 2 * tq, True)

    @pl.when(i % 2 == 0)
    def _():
        kv_step(pl.multiple_of(i * tq, tq), tq, True)

    acc = acc_scr[...]
    o12 = acc[:, :LANES] / acc[:, LANES:]
    lam = _lambda(lamp_ref, lam_init)
    o_ref[...] = _attn_finish(o12[:tq], o12[tq:], lam, lam_init, gsub_ref[...])


def _attn(cst, lamp, gsub, qkv, layer, *, nb, t, nh, tq):
    del layer
    bt, d = qkv.shape[0], nh * LANES
    nq = t // tq

    def kv(group):
        return pl.BlockSpec((t, LANES), lambda b, h, i: (b, group * nh + h))

    return pl.pallas_call(
        functools.partial(_attn_kernel, tq=tq, hd=LANES // 2),
        grid=(nb, nh, nq),
        in_specs=[pl.BlockSpec(memory_space=pltpu.SMEM),
                  pl.BlockSpec(lamp.shape, lambda b, h, i: (0, 0)),
                  pl.BlockSpec((1, LANES), lambda b, h, i: (0, 0)),
                  pl.BlockSpec((tq, LANES), lambda b, h, i: (b * nq + i, h)),
                  kv(1), kv(2)],
        out_specs=pl.BlockSpec((tq, LANES), lambda b, h, i: (b * nq + i, h)),
        out_shape=jax.ShapeDtypeStruct((bt, d), F32),
        scratch_shapes=[pltpu.VMEM((2 * tq, LANES), BF16), pltpu.VMEM((t, 2 * LANES), BF16),
                        pltpu.VMEM((2 * tq, LANES), F32), pltpu.VMEM((2 * tq, 2 * LANES), F32)],
        compiler_params=_params("parallel", "parallel", "arbitrary"),
        name="attn",
    )(cst, lamp, gsub, qkv, qkv, qkv)


def _attn_cached_kernel(cst_ref, lamp_ref, gsub_ref, q_ref, kc_ref, vc_ref, kn_ref, vn_ref, o_ref,
                        *, tq, past, hd):
    lam_init = cst_ref[0]
    lam = _lambda(lamp_ref, lam_init)
    nh = q_ref.shape[1] // LANES
    masked = past % CHUNK + tq > CHUNK
    for h in range(nh):
        cols = slice(h * LANES, (h + 1) * LANES)
        head = pl.ds(h, past, stride=nh)
        qs = _split_maps(q_ref[:, cols], hd)

        def scores(k, k0):
            s = lax.dot_general(qs, k.astype(BF16), _NT, preferred_element_type=F32)
            if masked:
                s = jnp.where(_visible(2 * tq, k.shape[0], tq, past, k0), s, MASK_VALUE)
            return s

        sc = scores(kc_ref[head, :], 0)
        sn = scores(kn_ref[:, cols], past)
        m = jnp.maximum(jnp.max(sc, axis=-1, keepdims=True), jnp.max(sn, axis=-1, keepdims=True))
        pc = jnp.exp2(sc - m)
        pn = jnp.exp2(sn - m)
        l = jnp.sum(pc, axis=-1, keepdims=True) + jnp.sum(pn, axis=-1, keepdims=True)
        acc = (jnp.dot(pc.astype(BF16), vc_ref[head, :].astype(BF16), preferred_element_type=F32)
               + jnp.dot(pn.astype(BF16), vn_ref[:, cols], preferred_element_type=F32))
        o12 = acc / l
        o_ref[:, cols] = _attn_finish(o12[:tq], o12[tq:], lam, lam_init, gsub_ref[...])


def _attn_cached(cst, lamp, gsub, qkv, ck, cv, layer, *, nb, t, nh):
    bt, d = qkv.shape[0], nh * LANES
    past = ck.shape[2]
    ck, cv = (c.reshape(c.shape[0], nb, past * nh, LANES) for c in (ck, cv))
    cache = pl.BlockSpec((None, None, past * nh, LANES), lambda b: (layer, b, 0, 0))

    def rows(group):
        return pl.BlockSpec((t, d), lambda b: (b, group))

    return pl.pallas_call(
        functools.partial(_attn_cached_kernel, tq=t, past=past, hd=LANES // 2),
        grid=(nb,),
        in_specs=[pl.BlockSpec(memory_space=pltpu.SMEM),
                  pl.BlockSpec(lamp.shape, lambda b: (0, 0)),
                  pl.BlockSpec((1, LANES), lambda b: (0, 0)),
                  rows(0), cache, cache, rows(1), rows(2)],
        out_specs=pl.BlockSpec((t, d), lambda b: (b, 0)),
        out_shape=jax.ShapeDtypeStruct((bt, d), F32),
        compiler_params=_params("parallel"),
        name="attn_cached",
    )(cst, lamp, gsub, qkv, ck, cv, qkv, qkv)


def _split3(x):
    hi = x.astype(BF16)
    r = x - hi.astype(F32)
    mid = r.astype(BF16)
    lo = (r - mid.astype(F32)).astype(BF16)
    return hi, mid, lo


def _hgrn_kernel(*refs, heads, tt, blk, has_s0):
    n_in = 10 if has_s0 else 9
    for h in range(heads):
        cols = pl.ds(h * LANES, LANES)
        per_head = [r.at[:, cols] for r in refs[:5]] + list(refs[5:9])
        if has_s0:
            per_head.append(refs[9].at[h])
        o_ref, sf_ref, st_scr, pad_scr = refs[n_in:]
        _hgrn_head(*per_head, o_ref.at[:, cols], sf_ref.at[h], st_scr.at[h], pad_scr.at[h],
                   tt=tt, blk=blk, has_s0=has_s0)


def _hgrn_head(*refs, tt, blk, has_s0):
    hq_ref, hf_ref, hi_ref, hg_ref, lb_ref, gn_ref, tri_ref, ones2_ref, lvl_ref = refs[:9]
    s0_ref = refs[9] if has_s0 else None
    o_ref, sf_ref, st_scr, pad_scr = refs[-4:]
    t = pl.program_id(2)

    @pl.when(t == 0)
    def _():
        if has_s0:
            st_scr[...] = s0_ref[...].T
        else:
            st_scr[...] = jnp.zeros_like(st_scr)
        pad_scr[:, 0:SUB, :] = jnp.zeros((3, SUB, LANES), F32)

    lb = lb_ref[...]
    sig = jax.nn.sigmoid(hf_ref[...])
    logf = jnp.log(jnp.maximum(lb + (1.0 - lb) * sig, TINY)) * math.log2(math.e)
    kk = (1.0 - lb) * (1.0 - sig)
    qq = _silu(hq_ref[...])
    vv = hi_ref[...]

    tri = tri_ref[...]
    parts = _split3(logf)
    b = jnp.concatenate(
        [sum(jnp.dot(tri, part[c * blk:(c + 1) * blk], preferred_element_type=F32)
             for part in parts) for c in range(tt // blk)], axis=0)

    pad_scr[0, SUB:, :] = kk
    pad_scr[1, SUB:, :] = b
    pad_scr[2, SUB:, :] = vv
    ones2 = ones2_ref[...]
    pos = lax.broadcasted_iota(jnp.int32, (SUB, LANES), 0)

    def lag_terms(d):
        if d == 0:
            return (qq * kk).astype(BF16), vv
        ks, bs, vs = (pad_scr[r, SUB - d:SUB - d + tt, :] for r in range(3))
        off = jnp.tile(jnp.where(pos >= d, 0.0, MASK_VALUE), (tt // SUB, 1))
        return (qq * ks * jnp.exp2(b - bs + off)).astype(BF16), vs

    o = jnp.zeros((tt, LANES), F32)
    for d in range(0, SUB, 2):
        (p0, v0), (p1, v1) = lag_terms(d), lag_terms(d + 1)
        r = jnp.dot(jnp.concatenate([p0, p1], axis=1), ones2, preferred_element_type=F32)
        o = o + r[:, :LANES] * v0 + r[:, LANES:] * v1

    st = st_scr[...]
    older = []
    for c in range(tt // blk):
        sl = slice(c * blk, (c + 1) * blk)
        bc = b[sl]
        b_last = bc[blk - 1:blk]
        qt = (qq[sl] * jnp.exp2(bc)).astype(BF16)
        older.append(lax.dot_general(qt, st.astype(BF16), _NT, preferred_element_type=F32))
        kt = (kk[sl] * jnp.exp2(b_last - bc)).astype(BF16)
        st = st * jnp.exp2(b_last) + lax.dot_general(vv[sl].astype(BF16), kt, _TN,
                                                    preferred_element_type=F32)
    st_scr[...] = st
    o = o + jnp.concatenate(older, axis=0)

    weights = [None] * (tt // blk)
    m, level = SUB, 0
    while 2 * m <= blk:
        b3 = b.reshape(tt // (2 * m), 2 * m, LANES)
        w = jnp.exp2(-jnp.abs(b3 - b3[:, m - 1:m, :])).reshape(tt, LANES)
        qt = (qq * w).astype(BF16)
        kt = (kk * w).astype(BF16)
        for c in range(tt // blk):
            sl = slice(c * blk, (c + 1) * blk)
            a = lax.dot_general(qt[sl], kt[sl], _NT, preferred_element_type=F32) * lvl_ref[level]
            weights[c] = a if weights[c] is None else weights[c] + a
        m, level = 2 * m, level + 1
    if level:
        o = o + jnp.concatenate(
            [jnp.dot(weights[c].astype(BF16), vv[c * blk:(c + 1) * blk].astype(BF16),
                     preferred_element_type=F32) for c in range(tt // blk)], axis=0)

    y = o * lax.rsqrt(jnp.mean(o * o, axis=-1, keepdims=True) + EPS) * gn_ref[...]
    o_ref[...] = y * _silu(hg_ref[...])

    @pl.when(t == pl.num_programs(2) - 1)
    def _():
        sf_ref[...] = st.T


def _hgrn(z, lb, gn, s0, layer, *, nb, t, nh, tt, blk, heads):
    bt = z.shape[0]
    nt = t // tt
    has_s0 = s0 is not None
    ng = nh // heads
    wide = heads * LANES

    def col(group):
        return pl.BlockSpec((tt, wide), lambda b, h, i: (b * nt + i, group * ng + h))

    state = pl.BlockSpec((None, heads, LANES, LANES), lambda b, h, i: (b, h, 0, 0))
    tri = jnp.tril(jnp.ones((blk, blk), BF16))
    ones2 = jnp.kron(jnp.eye(2, dtype=BF16), jnp.ones((LANES, LANES), BF16))
    pos = jnp.arange(blk)
    sizes = [SUB << i for i in range(max(blk // SUB, 1).bit_length() - 1)]
    lvl = jnp.stack([((pos[:, None] // (2 * m) == pos[None, :] // (2 * m))
                      & (pos[:, None] % (2 * m) >= m) & (pos[None, :] % (2 * m) < m))
                     for m in sizes]).astype(F32) if sizes else jnp.zeros((1, blk, blk), F32)
    in_specs = [col(0), col(1), col(2), col(3),
                pl.BlockSpec((1, wide), lambda b, h, i: (0, h)),
                pl.BlockSpec((1, LANES), lambda b, h, i: (0, 0)),
                _resident(tri.shape), _resident(ones2.shape), _resident(lvl.shape)]
    args = [z, z, z, z, lb, gn, tri, ones2, lvl]
    if has_s0:
        in_specs.append(pl.BlockSpec((None, None, heads, LANES, LANES),
                                     lambda b, h, i: (layer, b, h, 0, 0)))
        args.append(s0)
    return pl.pallas_call(
        functools.partial(_hgrn_kernel, heads=heads, tt=tt, blk=blk, has_s0=has_s0),
        grid=(nb, ng, nt),
        in_specs=in_specs,
        out_specs=[pl.BlockSpec((tt, wide), lambda b, h, i: (b * nt + i, h)), state],
        out_shape=[jax.ShapeDtypeStruct((bt, nh * LANES), F32),
                   jax.ShapeDtypeStruct((nb, nh, LANES, LANES), F32)],
        scratch_shapes=[pltpu.VMEM((heads, LANES, LANES), F32),
                        pltpu.VMEM((heads, 3, SUB + tt, LANES), F32)],
        compiler_params=_params("parallel", "parallel", "arbitrary"),
        name="hgrn",
    )(*args)


def _merge_kernel(x_ref, gt_ref, oa_ref, oh_ref, za_ref, zh_ref, wa_ref, wh_ref, wo_ref, o_ref):
    a = jnp.dot(oa_ref[...].astype(BF16), wa_ref[...], preferred_element_type=F32)
    r = jnp.dot(oh_ref[...].astype(BF16), wh_ref[...], preferred_element_type=F32)
    merged = jax.nn.sigmoid(za_ref[...]) * a + jax.nn.sigmoid(zh_ref[...]) * r
    o_ref[...] = x_ref[...] + gt_ref[...] * jnp.dot(merged.astype(BF16), wo_ref[...],
                                                    preferred_element_type=F32)


def _merge(x, gt, oa, oh, z, wa, wh, wo, *, tm, per_row, rows_per_seq):
    t, d = x.shape
    mod = _mod_spec(per_row, tm, d, rows_per_seq)
    row = pl.BlockSpec((tm, d), lambda i: (i, 0))
    return pl.pallas_call(
        _merge_kernel,
        grid=(t // tm,),
        in_specs=[row, mod, row, row,
                  pl.BlockSpec((tm, d), lambda i: (i, 4)), pl.BlockSpec((tm, d), lambda i: (i, 5)),
                  _resident(wa.shape), _resident(wh.shape), _resident(wo.shape)],
        out_specs=row,
        out_shape=jax.ShapeDtypeStruct((t, d), F32),
        compiler_params=_params("parallel"),
        name="merge",
    )(x, gt, oa, oh, z, z, wa, wh, wo)


def _tile(n, target):
    if n <= target:
        return n
    for c in range(target, 7, -1):
        if n % c == 0 and c % 8 == 0:
            return c
    return n


def _layer_weights(p, l, d, fc):
    f = p["w_ffn1_d"].shape[1]
    nc = f // fc

    def ffn(w_gu, w_d):
        wg = w_gu[l, :, :f].reshape(d, nc, fc).transpose(1, 0, 2).astype(BF16)
        wu = w_gu[l, :, f:].reshape(d, nc, fc).transpose(1, 0, 2).astype(BF16)
        return wg, wu, w_d[l].reshape(nc, fc, d).astype(BF16)

    ng = p["w_in"].shape[2] // d
    return dict(
        ffn1=ffn(p["w_ffn1_gu"], p["w_ffn1_d"]),
        ffn2=ffn(p["w_ffn2_gu"], p["w_ffn2_d"]),
        w_in=p["w_in"][l].reshape(d, ng, d).transpose(1, 0, 2).astype(BF16),
        w_ba=p["w_br_att"][l].astype(BF16),
        w_bh=p["w_br_hg"][l].astype(BF16),
        w_out=p["w_out"][l].astype(BF16),
    )


def _trunk(x3, mods, p, lb_all, weights, cache):
    nb, t, d = x3.shape
    depth = p["w_ada"].shape[0]
    nh = d // LANES
    bt = nb * t
    x = x3.reshape(bt, d)
    per_row = t < 256
    tm = _tile(bt if per_row else t, 512)
    kst = jnp.zeros((depth, bt * nh, LANES), F32)
    vst = jnp.zeros((depth, bt * nh, LANES), F32)
    states = []
    for l in range(depth):
        w = weights[l]
        m = mods[l]
        if per_row:
            mm = [jnp.repeat(m[:, i], t, axis=0) for i in range(N_ADA)]
        else:
            mm = [m[:, i].reshape(nb, 1, d) for i in range(N_ADA)]
        sh1, sc1, g1, sh2, sc2, g2, sh3, sc3, g3 = mm
        kw = dict(tm=tm, per_row=per_row, rows_per_seq=t)
        row = lambda a: a.reshape(1, -1)

        g_final = row(p["g_final"])
        x = _ffn(x, row(p["g_ffn1"][l]), sh1, sc1, g1, *w["ffn1"], g_final, final_norm=False, **kw)
        q_scale = (LANES // 2) ** -0.5 * math.log2(math.e)
        kw_in = dict(kw, tm=_tile(tm, 256))
        z, qkv, kst, vst = _inproj(x, row(p["g_mix"][l]), sh2, sc2, w["w_in"], kst, vst, l,
                                   q_scale=q_scale, **kw_in)

        lam_init = 0.8 - 0.6 * math.exp(-0.3 * l)
        cst = jnp.array([lam_init], F32)
        gsub = row(p["g_att_sub"][l])
        if cache is None:
            oa = _attn(cst, p["att_lambda"][l], gsub, qkv, l, nb=nb, t=t, nh=nh, tq=_tile(t, 512))
            s0 = None
        else:
            ck, cv, s0 = cache
            oa = _attn_cached(cst, p["att_lambda"][l], gsub, qkv, ck, cv, l, nb=nb, t=t, nh=nh)
        tt = _tile(t, 2048)
        blk = min(tt, 128)
        oh, s_fin = _hgrn(z, row(lb_all[l]), row(p["g_hg_norm"][l]), s0, l,
                          nb=nb, t=t, nh=nh, tt=tt, blk=blk, heads=nh if per_row else 1)
        states.append(s_fin)

        x = _merge(x, g2, oa, oh, z, w["w_ba"], w["w_bh"], w["w_out"], **kw)
        x = _ffn(x, row(p["g_ffn2"][l]), sh3, sc3, g3, *w["ffn2"], g_final,
                 final_norm=l == depth - 1, **kw)

    y = x.reshape(nb, t, d)
    shape5 = (depth, nb, t, nh, LANES)
    return y, kst.reshape(shape5), vst.reshape(shape5), jnp.stack(states)


def kernel(x_prompt, x_sample, cache_k, cache_v, state_hgrn, c_prompt, c_sample, w_ada, b_ada, g_ffn1, w_ffn1_gu, w_ffn1_d, g_mix, w_in, att_lambda, g_att_sub, hg_lb_logits, g_hg_norm, w_br_att, w_br_hg, w_out, g_ffn2, w_ffn2_gu, w_ffn2_d, g_final):
    p = dict(w_ada=w_ada, b_ada=b_ada, g_ffn1=g_ffn1, w_ffn1_gu=w_ffn1_gu, w_ffn1_d=w_ffn1_d,
             g_mix=g_mix, w_in=w_in, att_lambda=att_lambda, g_att_sub=g_att_sub,
             g_hg_norm=g_hg_norm, w_br_att=w_br_att, w_br_hg=w_br_hg, w_out=w_out,
             g_ffn2=g_ffn2, w_ffn2_gu=w_ffn2_gu, w_ffn2_d=w_ffn2_d, g_final=g_final)
    depth, d, _ = w_ada.shape
    nbp, nbs = x_prompt.shape[0], x_sample.shape[0]

    lb_sm = jax.nn.softmax(hg_lb_logits.astype(F32), axis=0)
    lb_all = jnp.cumsum(lb_sm, axis=0) - lb_sm[0]

    ns = nbp + nbs
    nsp = -(-ns // 8) * 8
    c_all = jnp.concatenate([c_prompt, c_sample, jnp.zeros((nsp - ns, d), F32)], axis=0)
    mods = _ada(c_all, w_ada, b_ada).reshape(depth, nsp, N_ADA, d)

    f = w_ffn1_d.shape[1]
    fc = 256 if f % 256 == 0 else LANES
    weights = [_layer_weights(p, l, d, fc) for l in range(depth)]

    y_p, k_p, v_p, s_p = _trunk(x_prompt, mods[:, :nbp], p, lb_all, weights, None)
    y_s, k_s, v_s, s_s = _trunk(x_sample, mods[:, nbp:ns], p, lb_all, weights,
                                (cache_k, cache_v, state_hgrn))
    return (y_p, y_s, k_p, v_p, s_p, k_s, v_s, s_s)
```

```python
import functools
import math

import jax
import jax.numpy as jnp
from jax import lax
from jax.experimental import pallas as pl
from jax.experimental.pallas import tpu as pltpu

F32 = jnp.float32
BF16 = jnp.bfloat16

CHUNK = 64
N_ADA = 9
MACARON_WEIGHT = 0.5
EPS = 1e-6
MASK_VALUE = -1e30
TINY = 1e-30
LANES = 128
SUB = 4
PAD = 8
VMEM_LIMIT = 56 * 1024 * 1024

_NT = (((1,), (1,)), ((), ()))
_TN = (((0,), (0,)), ((), ()))


def _params(*sem):
    return pltpu.CompilerParams(dimension_semantics=sem, vmem_limit_bytes=VMEM_LIMIT)


def _resident(shape):
    nd = len(shape)
    return pl.BlockSpec(shape, lambda *_: (0,) * nd, pipeline_mode=pl.Buffered(1))


def _modnorm(x, g, sc, sh):
    y = x * lax.rsqrt(jnp.mean(x * x, axis=-1, keepdims=True) + EPS) * g
    return y * (1.0 + sc) + sh


def _silu(x):
    return x * jax.nn.sigmoid(x)


def _mod_spec(per_row, tm, d, rows_per_seq):
    if per_row:
        return pl.BlockSpec((tm, d), lambda i, *_: (i, 0))
    tiles = rows_per_seq // tm
    return pl.BlockSpec((None, 1, d), lambda i, *_: (i // tiles, 0, 0))


def _ada_kernel(c_ref, w_ref, b_ref, o_ref):
    c = c_ref[...]
    o_ref[...] = jnp.dot(_silu(c).astype(BF16), w_ref[...].astype(BF16),
                         preferred_element_type=F32) + b_ref[...]


def _ada(c_all, w_ada, b_ada):
    depth, d, _ = w_ada.shape
    ns = c_all.shape[0]
    return pl.pallas_call(
        _ada_kernel,
        grid=(depth, N_ADA),
        in_specs=[pl.BlockSpec((ns, d), lambda l, j: (0, 0)),
                  pl.BlockSpec((None, d, d), lambda l, j: (l, 0, j)),
                  pl.BlockSpec((None, 1, d), lambda l, j: (l, 0, j))],
        out_specs=pl.BlockSpec((None, ns, d), lambda l, j: (l, 0, j)),
        out_shape=jax.ShapeDtypeStruct((depth, ns, N_ADA * d), F32),
        compiler_params=_params("parallel", "parallel"),
        name="ada",
    )(c_all, w_ada, b_ada.reshape(depth, 1, N_ADA * d))


def _ffn_kernel(x_ref, g_ref, sh_ref, sc_ref, gt_ref, wg_ref, wu_ref, wd_ref, gf_ref, o_ref,
                h_scr, acc_scr, *, n_chunks, final_norm):
    h_scr[...] = _modnorm(x_ref[...], g_ref[...], sc_ref[...], sh_ref[...]).astype(BF16)
    for c in range(n_chunks):
        h = h_scr[...]
        gate = jnp.dot(h, wg_ref[c], preferred_element_type=F32)
        up = jnp.dot(h, wu_ref[c], preferred_element_type=F32)
        a = (_silu(gate) * up).astype(BF16)
        down = jnp.dot(a, wd_ref[c], preferred_element_type=F32)
        if c == 0:
            acc_scr[...] = down
        else:
            acc_scr[...] += down
    y = x_ref[...] + MACARON_WEIGHT * gt_ref[...] * acc_scr[...]
    if final_norm:
        y = y * lax.rsqrt(jnp.mean(y * y, axis=-1, keepdims=True) + EPS) * gf_ref[...]
    o_ref[...] = y


def _ffn(x, g, sh, sc, gt, wg, wu, wd, g_final, *, tm, per_row, rows_per_seq, final_norm):
    t, d = x.shape
    n_chunks = wg.shape[0]
    mod = _mod_spec(per_row, tm, d, rows_per_seq)
    row = pl.BlockSpec((tm, d), lambda i: (i, 0))
    gain = pl.BlockSpec((1, d), lambda i: (0, 0))
    return pl.pallas_call(
        functools.partial(_ffn_kernel, n_chunks=n_chunks, final_norm=final_norm),
        grid=(t // tm,),
        in_specs=[row, gain, mod, mod, mod,
                  _resident(wg.shape), _resident(wu.shape), _resident(wd.shape), gain],
        out_specs=row,
        out_shape=jax.ShapeDtypeStruct((t, d), F32),
        scratch_shapes=[pltpu.VMEM((tm, d), BF16), pltpu.VMEM((tm, d), F32)],
        compiler_params=_params("parallel"),
        name="ffn",
    )(x, g, sh, sc, gt, wg, wu, wd, g_final)


N_QKV = 3


def _inproj_kernel(x_ref, g_ref, sh_ref, sc_ref, w_ref, kin_ref, vin_ref,
                   z_ref, qkv_ref, k_ref, v_ref, u_scr, *, q_scale):
    del kin_ref, vin_ref
    d = u_scr.shape[1]

    def group(g):
        return jnp.dot(u_scr[...], w_ref[g], preferred_element_type=F32)

    def cache_rows(ref, z):
        tm, nh = z.shape[0], z.shape[1] // LANES
        for h in range(nh):
            ref[pl.ds(h, tm, stride=nh), :] = z[:, h * LANES:(h + 1) * LANES]

    u_scr[...] = _modnorm(x_ref[...], g_ref[...], sc_ref[...], sh_ref[...]).astype(BF16)
    qkv_ref[:, 0:d] = (group(0) * q_scale).astype(BF16)
    for g, ref in ((1, k_ref), (2, v_ref)):
        z = group(g)
        cache_rows(ref, z)
        qkv_ref[:, g * d:(g + 1) * d] = z.astype(BF16)
    for g in range(N_QKV, w_ref.shape[0]):
        z_ref[:, (g - N_QKV) * d:(g - N_QKV + 1) * d] = group(g)


def _inproj(x, g, sh, sc, w3, kst, vst, layer, *, tm, per_row, rows_per_seq, q_scale):
    t, d = x.shape
    ng = w3.shape[0]
    mod = _mod_spec(per_row, tm, d, rows_per_seq)
    row = pl.BlockSpec((tm, d), lambda i: (i, 0))
    stack = pl.BlockSpec((None, tm * (d // LANES), LANES), lambda i: (layer, i, 0))
    anyspec = pl.BlockSpec(memory_space=pl.ANY)
    return pl.pallas_call(
        functools.partial(_inproj_kernel, q_scale=q_scale),
        grid=(t // tm,),
        in_specs=[row, pl.BlockSpec((1, d), lambda i: (0, 0)), mod, mod,
                  _resident(w3.shape), anyspec, anyspec],
        out_specs=[pl.BlockSpec((tm, (ng - N_QKV) * d), lambda i: (i, 0)),
                   pl.BlockSpec((tm, N_QKV * d), lambda i: (i, 0)),
                   stack, stack],
        out_shape=[jax.ShapeDtypeStruct((t, (ng - N_QKV) * d), F32),
                   jax.ShapeDtypeStruct((t, N_QKV * d), BF16),
                   jax.ShapeDtypeStruct(kst.shape, F32),
                   jax.ShapeDtypeStruct(vst.shape, F32)],
        scratch_shapes=[pltpu.VMEM((tm, d), BF16)],
        input_output_aliases={5: 2, 6: 3},
        compiler_params=_params("parallel"),
        name="inproj",
    )(x, g, sh, sc, w3, kst, vst)


def _lambda(lamp_ref, lam_init):
    lp = lamp_ref[...]
    s01 = jnp.sum(lp[0:1] * lp[1:2], axis=-1, keepdims=True)
    s23 = jnp.sum(lp[2:3] * lp[3:4], axis=-1, keepdims=True)
    return jnp.exp(s01) - jnp.exp(s23) + lam_init


def _split_maps(q, hd):
    lane = lax.broadcasted_iota(jnp.int32, q.shape, 1)
    zero = jnp.zeros_like(q)
    return jnp.concatenate([jnp.where(lane < hd, q, zero), jnp.where(lane >= hd, q, zero)],
                           axis=0).astype(BF16)


def _visible(rows, cols, tq, q0, k0):
    r = lax.broadcasted_iota(jnp.int32, (rows, cols), 0)
    qpos = q0 + jnp.where(r >= tq, r - tq, r)
    kpos = k0 + lax.broadcasted_iota(jnp.int32, (rows, cols), 1)
    return kpos < (qpos // CHUNK + 1) * CHUNK


def _attn_finish(o1, o2, lam, lam_init, gsub):
    o = o1 - lam * o2
    y = o * lax.rsqrt(jnp.mean(o * o, axis=-1, keepdims=True) + EPS) * gsub
    return (y * (1.0 - lam_init)).astype(BF16)


def _attn_kernel(cst_ref, lamp_ref, gsub_ref, q_ref, k_ref, v_ref, o_ref,
                 qs_scr, vx_scr, m_scr, acc_scr, *, tq, hd):
    i = pl.program_id(2)
    lam_init = cst_ref[0]

    @pl.when(i == 0)
    def _():
        vx_scr[:, :LANES] = v_ref[...]
        vx_scr[:, LANES:] = jnp.ones((vx_scr.shape[0], LANES), BF16)

    q = q_ref[...]
    lane = lax.broadcasted_iota(jnp.int32, q.shape, 1)
    qs_scr[:tq] = jnp.where(lane < hd, q, jnp.zeros_like(q))
    qs_scr[tq:] = jnp.where(lane >= hd, q, jnp.zeros_like(q))
    m_scr[...] = jnp.full_like(m_scr, MASK_VALUE)
    acc_scr[...] = jnp.zeros_like(acc_scr)

    def kv_step(start, width, masked):
        k = k_ref[pl.ds(start, width), :]
        vx = vx_scr[pl.ds(start, width), :]
        if masked:
            vis = _visible(tq, width, tq, i * tq, start)
        maps = (slice(0, tq), slice(tq, 2 * tq))
        scores = [lax.dot_general(qs_scr[rows], k, _NT, preferred_element_type=F32)
                  for rows in maps]
        for rows, s in zip(maps, scores):
            if masked:
                s = jnp.where(vis, s, MASK_VALUE)
            m_old = m_scr[rows]
            m_new = jnp.maximum(m_old, jnp.max(s, axis=-1, keepdims=True))
            p = jnp.exp2(s - jnp.tile(m_new, (1, width // LANES))).astype(BF16)
            alpha = jnp.exp2(m_old - m_new)
            acc_scr[rows] = (jnp.tile(alpha, (1, 2)) * acc_scr[rows]
                             + jnp.dot(p, vx, preferred_element_type=F32))
            m_scr[rows] = m_new

    def full_step(j):
        kv_step(pl.multiple_of(j * 2 * tq, 2 * tq), 2 * tq, False)

    def full_body(k, carry):
        full_step(2 * k)
        full_step(2 * k + 1)
        return carry

    n_full = i // 2
    lax.fori_loop(0, n_full // 2, full_body, 0)

    @pl.when(n_full % 2 == 1)
    def _():
        full_step(n_full - 1)

    @pl.when(i % 2 == 1)
    def _():
        kv_step(pl.multiple_of((i - 1) * tq, tq), 2 * tq, True)

    @pl.when(i % 2 == 0)
    def _():
        kv_step(pl.multiple_of(i * tq, tq), tq, True)

    acc = acc_scr[...]
    o12 = acc[:, :LANES] / acc[:, LANES:]
    lam = _lambda(lamp_ref, lam_init)
    o_ref[...] = _attn_finish(o12[:tq], o12[tq:], lam, lam_init, gsub_ref[...])


def _attn(cst, lamp, gsub, qkv, layer, *, nb, t, nh, tq):
    del layer
    bt, d = qkv.shape[0], nh * LANES
    nq = t // tq

    def kv(group):
        return pl.BlockSpec((t, LANES), lambda b, h, i: (b, group * nh + h))

    return pl.pallas_call(
        functools.partial(_attn_kernel, tq=tq, hd=LANES // 2),
        grid=(nb, nh, nq),
        in_specs=[pl.BlockSpec(memory_space=pltpu.SMEM),
                  pl.BlockSpec(lamp.shape, lambda b, h, i: (0, 0)),
                  pl.BlockSpec((1, LANES), lambda b, h, i: (0, 0)),
                  pl.BlockSpec((tq, LANES), lambda b, h, i: (b * nq + i, h)),
                  kv(1), kv(2)],
        out_specs=pl.BlockSpec((tq, LANES), lambda b, h, i: (b * nq + i, h)),
        out_shape=jax.ShapeDtypeStruct((bt, d), BF16),
        scratch_shapes=[pltpu.VMEM((2 * tq, LANES), BF16), pltpu.VMEM((t, 2 * LANES), BF16),
                        pltpu.VMEM((2 * tq, LANES), F32), pltpu.VMEM((2 * tq, 2 * LANES), F32)],
        compiler_params=_params("parallel", "parallel", "arbitrary"),
        name="attn",
    )(cst, lamp, gsub, qkv, qkv, qkv)


def _attn_cached_kernel(cst_ref, lamp_ref, gsub_ref, q_ref, kc_ref, vc_ref, kn_ref, vn_ref, o_ref,
                        *, tq, past, hd):
    lam_init = cst_ref[0]
    lam = _lambda(lamp_ref, lam_init)
    nh = q_ref.shape[1] // LANES
    masked = past % CHUNK + tq > CHUNK
    for h in range(nh):
        cols = slice(h * LANES, (h + 1) * LANES)
        head = pl.ds(h, past, stride=nh)
        qs = _split_maps(q_ref[:, cols], hd)

        def scores(k, k0):
            s = lax.dot_general(qs, k.astype(BF16), _NT, preferred_element_type=F32)
            if masked:
                s = jnp.where(_visible(2 * tq, k.shape[0], tq, past, k0), s, MASK_VALUE)
            return s

        sc = scores(kc_ref[head, :], 0)
        sn = scores(kn_ref[:, cols], past)
        m = jnp.maximum(jnp.max(sc, axis=-1, keepdims=True), jnp.max(sn, axis=-1, keepdims=True))
        pc = jnp.exp2(sc - m)
        pn = jnp.exp2(sn - m)
        l = jnp.sum(pc, axis=-1, keepdims=True) + jnp.sum(pn, axis=-1, keepdims=True)
        acc = (jnp.dot(pc.astype(BF16), vc_ref[head, :].astype(BF16), preferred_element_type=F32)
               + jnp.dot(pn.astype(BF16), vn_ref[:, cols], preferred_element_type=F32))
        o12 = acc / l
        o_ref[:, cols] = _attn_finish(o12[:tq], o12[tq:], lam, lam_init, gsub_ref[...])


def _attn_cached(cst, lamp, gsub, qkv, ck, cv, layer, *, nb, t, nh):
    bt, d = qkv.shape[0], nh * LANES
    past = ck.shape[2]
    ck, cv = (c.reshape(c.shape[0], nb, past * nh, LANES) for c in (ck, cv))
    cache = pl.BlockSpec((None, None, past * nh, LANES), lambda b: (layer, b, 0, 0))

    def rows(group):
        return pl.BlockSpec((t, d), lambda b: (b, group))

    return pl.pallas_call(
        functools.partial(_attn_cached_kernel, tq=t, past=past, hd=LANES // 2),
        grid=(nb,),
        in_specs=[pl.BlockSpec(memory_space=pltpu.SMEM),
                  pl.BlockSpec(lamp.shape, lambda b: (0, 0)),
                  pl.BlockSpec((1, LANES), lambda b: (0, 0)),
                  rows(0), cache, cache, rows(1), rows(2)],
        out_specs=pl.BlockSpec((t, d), lambda b: (b, 0)),
        out_shape=jax.ShapeDtypeStruct((bt, d), BF16),
        compiler_params=_params("parallel"),
        name="attn_cached",
    )(cst, lamp, gsub, qkv, ck, cv, qkv, qkv)


def _split3(x):
    hi = x.astype(BF16)
    r = x - hi.astype(F32)
    mid = r.astype(BF16)
    lo = (r - mid.astype(F32)).astype(BF16)
    return hi, mid, lo


def _hgrn_kernel(*refs, heads, tt, blk, has_s0):
    n_in = 10 if has_s0 else 9
    for h in range(heads):
        cols = pl.ds(h * LANES, LANES)
        per_head = [r.at[:, cols] for r in refs[:5]] + list(refs[5:9])
        if has_s0:
            per_head.append(refs[9].at[h])
        o_ref, sf_ref, st_scr, pad_scr = refs[n_in:]
        _hgrn_head(*per_head, o_ref.at[:, cols], sf_ref.at[h], st_scr.at[h], pad_scr.at[h],
                   tt=tt, blk=blk, has_s0=has_s0)


def _hgrn_head(*refs, tt, blk, has_s0):
    hq_ref, hf_ref, hi_ref, hg_ref, lb_ref, gn_ref, tri_ref, ones2_ref, lvl_ref = refs[:9]
    s0_ref = refs[9] if has_s0 else None
    o_ref, sf_ref, st_scr, pad_scr = refs[-4:]
    t = pl.program_id(2)

    @pl.when(t == 0)
    def _():
        if has_s0:
            st_scr[...] = s0_ref[...].T
        else:
            st_scr[...] = jnp.zeros_like(st_scr)
        pad_scr[:, 0:PAD, :] = jnp.zeros((3, PAD, LANES), F32)

    lb = lb_ref[...]
    sig = jax.nn.sigmoid(hf_ref[...])
    logf = jnp.log(jnp.maximum(lb + (1.0 - lb) * sig, TINY)) * math.log2(math.e)
    kk = (1.0 - lb) * (1.0 - sig)
    qq = _silu(hq_ref[...])
    vv = hi_ref[...]

    tri = tri_ref[...]
    parts = _split3(logf)
    b = jnp.concatenate(
        [sum(jnp.dot(tri, part[c * blk:(c + 1) * blk], preferred_element_type=F32)
             for part in parts) for c in range(tt // blk)], axis=0)

    pad_scr[0, PAD:, :] = kk
    pad_scr[1, PAD:, :] = b
    pad_scr[2, PAD:, :] = vv
    ones2 = ones2_ref[...]
    pos = lax.broadcasted_iota(jnp.int32, (PAD, LANES), 0) % SUB

    def lag_terms(d):
        if d == 0:
            return (qq * kk).astype(BF16), vv
        ks, bs, vs = (pad_scr[r, PAD - d:PAD - d + tt, :] for r in range(3))
        off = jnp.tile(jnp.where(pos >= d, 0.0, MASK_VALUE), (tt // PAD, 1))
        return (qq * ks * jnp.exp2(b - bs + off)).astype(BF16), vs

    o = jnp.zeros((tt, LANES), F32)
    for d in range(0, SUB, 2):
        (p0, v0), (p1, v1) = lag_terms(d), lag_terms(d + 1)
        r = jnp.dot(jnp.concatenate([p0, p1], axis=1), ones2, preferred_element_type=F32)
        o = o + r[:, :LANES] * v0 + r[:, LANES:] * v1

    st = st_scr[...]
    older = []
    for c in range(tt // blk):
        sl = slice(c * blk, (c + 1) * blk)
        bc = b[sl]
        b_last = bc[blk - 1:blk]
        qt = (qq[sl] * jnp.exp2(bc)).astype(BF16)
        older.append(lax.dot_general(qt, st.astype(BF16), _NT, preferred_element_type=F32))
        kt = (kk[sl] * jnp.exp2(b_last - bc)).astype(BF16)
        st = st * jnp.exp2(b_last) + lax.dot_general(vv[sl].astype(BF16), kt, _TN,
                                                    preferred_element_type=F32)
    st_scr[...] = st
    o = o + jnp.concatenate(older, axis=0)

    weights = [None] * (tt // blk)
    m, level = SUB, 0
    while 2 * m <= blk:
        b3 = b.reshape(tt // (2 * m), 2 * m, LANES)
        w = jnp.exp2(-jnp.abs(b3 - b3[:, m - 1:m, :])).reshape(tt, LANES)
        qt = (qq * w).astype(BF16)
        kt = (kk * w).astype(BF16)
        for c in range(tt // blk):
            sl = slice(c * blk, (c + 1) * blk)
            a = lax.dot_general(qt[sl], kt[sl], _NT, preferred_element_type=F32) * lvl_ref[level]
            weights[c] = a if weights[c] is None else weights[c] + a
        m, level = 2 * m, level + 1
    if level:
        o = o + jnp.concatenate(
            [jnp.dot(weights[c].astype(BF16), vv[c * blk:(c + 1) * blk].astype(BF16),
                     preferred_element_type=F32) for c in range(tt // blk)], axis=0)

    y = o * lax.rsqrt(jnp.mean(o * o, axis=-1, keepdims=True) + EPS) * gn_ref[...]
    o_ref[...] = (y * _silu(hg_ref[...])).astype(BF16)

    @pl.when(t == pl.num_programs(2) - 1)
    def _():
        sf_ref[...] = st.T


def _hgrn(z, lb, gn, s0, layer, *, nb, t, nh, tt, blk, heads):
    bt = z.shape[0]
    nt = t // tt
    has_s0 = s0 is not None
    ng = nh // heads
    wide = heads * LANES

    def col(group):
        return pl.BlockSpec((tt, wide), lambda b, h, i: (b * nt + i, group * ng + h))

    state = pl.BlockSpec((None, heads, LANES, LANES), lambda b, h, i: (b, h, 0, 0))
    tri = jnp.tril(jnp.ones((blk, blk), BF16))
    ones2 = jnp.kron(jnp.eye(2, dtype=BF16), jnp.ones((LANES, LANES), BF16))
    pos = jnp.arange(blk)
    sizes = [SUB << i for i in range(max(blk // SUB, 1).bit_length() - 1)]
    lvl = jnp.stack([((pos[:, None] // (2 * m) == pos[None, :] // (2 * m))
                      & (pos[:, None] % (2 * m) >= m) & (pos[None, :] % (2 * m) < m))
                     for m in sizes]).astype(F32) if sizes else jnp.zeros((1, blk, blk), F32)
    in_specs = [col(0), col(1), col(2), col(3),
                pl.BlockSpec((1, wide), lambda b, h, i: (0, h)),
                pl.BlockSpec((1, LANES), lambda b, h, i: (0, 0)),
                _resident(tri.shape), _resident(ones2.shape), _resident(lvl.shape)]
    args = [z, z, z, z, lb, gn, tri, ones2, lvl]
    if has_s0:
        in_specs.append(pl.BlockSpec((None, None, heads, LANES, LANES),
                                     lambda b, h, i: (layer, b, h, 0, 0)))
        args.append(s0)
    return pl.pallas_call(
        functools.partial(_hgrn_kernel, heads=heads, tt=tt, blk=blk, has_s0=has_s0),
        grid=(nb, ng, nt),
        in_specs=in_specs,
        out_specs=[pl.BlockSpec((tt, wide), lambda b, h, i: (b * nt + i, h)), state],
        out_shape=[jax.ShapeDtypeStruct((bt, nh * LANES), BF16),
                   jax.ShapeDtypeStruct((nb, nh, LANES, LANES), F32)],
        scratch_shapes=[pltpu.VMEM((heads, LANES, LANES), F32),
                        pltpu.VMEM((heads, 3, PAD + tt, LANES), F32)],
        compiler_params=_params("parallel", "parallel", "arbitrary"),
        name="hgrn",
    )(*args)


def _merge_kernel(x_ref, gt_ref, oa_ref, oh_ref, za_ref, zh_ref, wa_ref, wh_ref, wo_ref, o_ref):
    a = jnp.dot(oa_ref[...], wa_ref[...], preferred_element_type=F32)
    r = jnp.dot(oh_ref[...], wh_ref[...], preferred_element_type=F32)
    merged = jax.nn.sigmoid(za_ref[...]) * a + jax.nn.sigmoid(zh_ref[...]) * r
    o_ref[...] = x_ref[...] + gt_ref[...] * jnp.dot(merged.astype(BF16), wo_ref[...],
                                                    preferred_element_type=F32)


def _merge(x, gt, oa, oh, z, wa, wh, wo, *, tm, per_row, rows_per_seq):
    t, d = x.shape
    mod = _mod_spec(per_row, tm, d, rows_per_seq)
    row = pl.BlockSpec((tm, d), lambda i: (i, 0))
    return pl.pallas_call(
        _merge_kernel,
        grid=(t // tm,),
        in_specs=[row, mod, row, row,
                  pl.BlockSpec((tm, d), lambda i: (i, 4)), pl.BlockSpec((tm, d), lambda i: (i, 5)),
                  _resident(wa.shape), _resident(wh.shape), _resident(wo.shape)],
        out_specs=row,
        out_shape=jax.ShapeDtypeStruct((t, d), F32),
        compiler_params=_params("parallel"),
        name="merge",
    )(x, gt, oa, oh, z, z, wa, wh, wo)


def _tile(n, target):
    if n <= target:
        return n
    for c in range(target, 7, -1):
        if n % c == 0 and c % 8 == 0:
            return c
    return n


def _layer_weights(p, l, d, fc):
    f = p["w_ffn1_d"].shape[1]
    nc = f // fc

    def ffn(w_gu, w_d):
        wg = w_gu[l, :, :f].reshape(d, nc, fc).transpose(1, 0, 2).astype(BF16)
        wu = w_gu[l, :, f:].reshape(d, nc, fc).transpose(1, 0, 2).astype(BF16)
        return wg, wu, w_d[l].reshape(nc, fc, d).astype(BF16)

    ng = p["w_in"].shape[2] // d
    return dict(
        ffn1=ffn(p["w_ffn1_gu"], p["w_ffn1_d"]),
        ffn2=ffn(p["w_ffn2_gu"], p["w_ffn2_d"]),
        w_in=p["w_in"][l].reshape(d, ng, d).transpose(1, 0, 2).astype(BF16),
        w_ba=p["w_br_att"][l].astype(BF16),
        w_bh=p["w_br_hg"][l].astype(BF16),
        w_out=p["w_out"][l].astype(BF16),
    )


def _trunk(x3, mods, p, lb_all, weights, cache):
    nb, t, d = x3.shape
    depth = p["w_ada"].shape[0]
    nh = d // LANES
    bt = nb * t
    x = x3.reshape(bt, d)
    per_row = t < 256
    tm = _tile(bt if per_row else t, 512)
    kst = jnp.zeros((depth, bt * nh, LANES), F32)
    vst = jnp.zeros((depth, bt * nh, LANES), F32)
    states = []
    for l in range(depth):
        w = weights[l]
        m = mods[l]
        if per_row:
            mm = [jnp.repeat(m[:, i], t, axis=0) for i in range(N_ADA)]
        else:
            mm = [m[:, i].reshape(nb, 1, d) for i in range(N_ADA)]
        sh1, sc1, g1, sh2, sc2, g2, sh3, sc3, g3 = mm
        kw = dict(tm=tm, per_row=per_row, rows_per_seq=t)
        row = lambda a: a.reshape(1, -1)

        g_final = row(p["g_final"])
        x = _ffn(x, row(p["g_ffn1"][l]), sh1, sc1, g1, *w["ffn1"], g_final, final_norm=False, **kw)
        q_scale = (LANES // 2) ** -0.5 * math.log2(math.e)
        kw_in = dict(kw, tm=_tile(tm, 256))
        z, qkv, kst, vst = _inproj(x, row(p["g_mix"][l]), sh2, sc2, w["w_in"], kst, vst, l,
                                   q_scale=q_scale, **kw_in)

        lam_init = 0.8 - 0.6 * math.exp(-0.3 * l)
        cst = jnp.array([lam_init], F32)
        gsub = row(p["g_att_sub"][l])
        if cache is None:
            oa = _attn(cst, p["att_lambda"][l], gsub, qkv, l, nb=nb, t=t, nh=nh, tq=_tile(t, 512))
            s0 = None
        else:
            ck, cv, s0 = cache
            oa = _attn_cached(cst, p["att_lambda"][l], gsub, qkv, ck, cv, l, nb=nb, t=t, nh=nh)
        tt = _tile(t, 2048)
        blk = min(tt, 128)
        oh, s_fin = _hgrn(z, row(lb_all[l]), row(p["g_hg_norm"][l]), s0, l,
                          nb=nb, t=t, nh=nh, tt=tt, blk=blk, heads=nh if per_row else 1)
        states.append(s_fin)

        x = _merge(x, g2, oa, oh, z, w["w_ba"], w["w_bh"], w["w_out"], **kw)
        x = _ffn(x, row(p["g_ffn2"][l]), sh3, sc3, g3, *w["ffn2"], g_final,
                 final_norm=l == depth - 1, **kw)

    y = x.reshape(nb, t, d)
    shape5 = (depth, nb, t, nh, LANES)
    return y, kst.reshape(shape5), vst.reshape(shape5), jnp.stack(states)


def kernel(x_prompt, x_sample, cache_k, cache_v, state_hgrn, c_prompt, c_sample, w_ada, b_ada, g_ffn1, w_ffn1_gu, w_ffn1_d, g_mix, w_in, att_lambda, g_att_sub, hg_lb_logits, g_hg_norm, w_br_att, w_br_hg, w_out, g_ffn2, w_ffn2_gu, w_ffn2_d, g_final):
    p = dict(w_ada=w_ada, b_ada=b_ada, g_ffn1=g_ffn1, w_ffn1_gu=w_ffn1_gu, w_ffn1_d=w_ffn1_d,
             g_mix=g_mix, w_in=w_in, att_lambda=att_lambda, g_att_sub=g_att_sub,
             g_hg_norm=g_hg_norm, w_br_att=w_br_att, w_br_hg=w_br_hg, w_out=w_out,
             g_ffn2=g_ffn2, w_ffn2_gu=w_ffn2_gu, w_ffn2_d=w_ffn2_d, g_final=g_final)
    depth, d, _ = w_ada.shape
    nbp, nbs = x_prompt.shape[0], x_sample.shape[0]

    lb_sm = jax.nn.softmax(hg_lb_logits.astype(F32), axis=0)
    lb_all = jnp.cumsum(lb_sm, axis=0) - lb_sm[0]

    ns = nbp + nbs
    nsp = -(-ns // 8) * 8
    c_all = jnp.concatenate([c_prompt, c_sample, jnp.zeros((nsp - ns, d), F32)], axis=0)
    mods = _ada(c_all, w_ada, b_ada).reshape(depth, nsp, N_ADA, d)

    f = w_ffn1_d.shape[1]
    fc = 256 if f % 256 == 0 else LANES
    weights = [_layer_weights(p, l, d, fc) for l in range(depth)]

    y_p, k_p, v_p, s_p = _trunk(x_prompt, mods[:, :nbp], p, lb_all, weights, None)
    y_s, k_s, v_s, s_s = _trunk(x_sample, mods[:, nbp:ns], p, lb_all, weights,
                                (cache_k, cache_v, state_hgrn))
    return (y_p, y_s, k_p, v_p, s_p, k_s, v_s, s_s)
```

```python
import functools
import math

import jax
import jax.numpy as jnp
from jax import lax
from jax.experimental import pallas as pl
from jax.experimental.pallas import tpu as pltpu

F32 = jnp.float32
BF16 = jnp.bfloat16

CHUNK = 64
N_ADA = 9
MACARON_WEIGHT = 0.5
EPS = 1e-6
MASK_VALUE = -1e30
TINY = 1e-30
LANES = 128
SUB = 4
PAD = 8
VMEM_LIMIT = 56 * 1024 * 1024
Q_TILE = 512
KV_TILE = 2048
KV_STEPS_PER_TRIP = 2

_NT = (((1,), (1,)), ((), ()))
_TN = (((0,), (0,)), ((), ()))


def _params(*sem):
    return pltpu.CompilerParams(dimension_semantics=sem, vmem_limit_bytes=VMEM_LIMIT)


def _resident(shape):
    nd = len(shape)
    return pl.BlockSpec(shape, lambda *_: (0,) * nd, pipeline_mode=pl.Buffered(1))


def _modnorm(x, g, sc, sh):
    y = x * lax.rsqrt(jnp.mean(x * x, axis=-1, keepdims=True) + EPS) * g
    return y * (1.0 + sc) + sh


def _silu(x):
    return x * jax.nn.sigmoid(x)


def _mod_spec(per_row, tm, d, rows_per_seq):
    if per_row:
        return pl.BlockSpec((tm, d), lambda i, *_: (i, 0))
    tiles = rows_per_seq // tm
    return pl.BlockSpec((None, 1, d), lambda i, *_: (i // tiles, 0, 0))


def _ada_kernel(c_ref, w_ref, b_ref, o_ref):
    c = c_ref[...]
    o_ref[...] = jnp.dot(_silu(c).astype(BF16), w_ref[...].astype(BF16),
                         preferred_element_type=F32) + b_ref[...]


def _ada(c_all, w_ada, b_ada):
    depth, d, _ = w_ada.shape
    ns = c_all.shape[0]
    wide = 3 * d
    return pl.pallas_call(
        _ada_kernel,
        grid=(depth, N_ADA * d // wide),
        in_specs=[pl.BlockSpec((ns, d), lambda l, j: (0, 0)),
                  pl.BlockSpec((None, d, wide), lambda l, j: (l, 0, j)),
                  pl.BlockSpec((None, 1, wide), lambda l, j: (l, 0, j))],
        out_specs=pl.BlockSpec((None, ns, wide), lambda l, j: (l, 0, j)),
        out_shape=jax.ShapeDtypeStruct((depth, ns, N_ADA * d), F32),
        compiler_params=_params("parallel", "parallel"),
        name="ada",
    )(c_all, w_ada, b_ada.reshape(depth, 1, N_ADA * d))


def _ffn_kernel(x_ref, g_ref, sh_ref, sc_ref, gt_ref, wg_ref, wu_ref, wd_ref, gf_ref, o_ref,
                h_scr, acc_scr, *, n_chunks, final_norm):
    h_scr[...] = _modnorm(x_ref[...], g_ref[...], sc_ref[...], sh_ref[...]).astype(BF16)
    for c in range(n_chunks):
        h = h_scr[...]
        gate = jnp.dot(h, wg_ref[c], preferred_element_type=F32)
        up = jnp.dot(h, wu_ref[c], preferred_element_type=F32)
        a = (_silu(gate) * up).astype(BF16)
        down = jnp.dot(a, wd_ref[c], preferred_element_type=F32)
        if c == 0:
            acc_scr[...] = down
        else:
            acc_scr[...] += down
    y = x_ref[...] + MACARON_WEIGHT * gt_ref[...] * acc_scr[...]
    if final_norm:
        y = y * lax.rsqrt(jnp.mean(y * y, axis=-1, keepdims=True) + EPS) * gf_ref[...]
    o_ref[...] = y


def _ffn(x, g, sh, sc, gt, wg, wu, wd, g_final, *, tm, per_row, rows_per_seq, final_norm):
    t, d = x.shape
    n_chunks = wg.shape[0]
    mod = _mod_spec(per_row, tm, d, rows_per_seq)
    row = pl.BlockSpec((tm, d), lambda i: (i, 0))
    gain = pl.BlockSpec((1, d), lambda i: (0, 0))
    return pl.pallas_call(
        functools.partial(_ffn_kernel, n_chunks=n_chunks, final_norm=final_norm),
        grid=(t // tm,),
        in_specs=[row, gain, mod, mod, mod,
                  _resident(wg.shape), _resident(wu.shape), _resident(wd.shape), gain],
        out_specs=row,
        out_shape=jax.ShapeDtypeStruct((t, d), F32),
        scratch_shapes=[pltpu.VMEM((tm, d), BF16), pltpu.VMEM((tm, d), F32)],
        compiler_params=_params("parallel"),
        name="ffn",
    )(x, g, sh, sc, gt, wg, wu, wd, g_final)


N_QKV = 3


def _inproj_kernel(x_ref, g_ref, sh_ref, sc_ref, w_ref, kin_ref, vin_ref,
                   z_ref, qkv_ref, k_ref, v_ref, u_scr, *, q_scale):
    del kin_ref, vin_ref
    d = u_scr.shape[1]

    def group(g):
        return jnp.dot(u_scr[...], w_ref[g], preferred_element_type=F32)

    def cache_rows(ref, z):
        tm, nh = z.shape[0], z.shape[1] // LANES
        for h in range(nh):
            ref[pl.ds(h, tm, stride=nh), :] = z[:, h * LANES:(h + 1) * LANES]

    u_scr[...] = _modnorm(x_ref[...], g_ref[...], sc_ref[...], sh_ref[...]).astype(BF16)
    qkv_ref[:, 0:d] = (group(0) * q_scale).astype(BF16)
    for g, ref in ((1, k_ref), (2, v_ref)):
        z = group(g)
        cache_rows(ref, z)
        qkv_ref[:, g * d:(g + 1) * d] = z.astype(BF16)
    for g in range(N_QKV, w_ref.shape[0]):
        z_ref[:, (g - N_QKV) * d:(g - N_QKV + 1) * d] = group(g)


def _inproj(x, g, sh, sc, w3, kst, vst, layer, *, tm, per_row, rows_per_seq, q_scale):
    t, d = x.shape
    ng = w3.shape[0]
    mod = _mod_spec(per_row, tm, d, rows_per_seq)
    row = pl.BlockSpec((tm, d), lambda i: (i, 0))
    stack = pl.BlockSpec((None, tm * (d // LANES), LANES), lambda i: (layer, i, 0))
    anyspec = pl.BlockSpec(memory_space=pl.ANY)
    return pl.pallas_call(
        functools.partial(_inproj_kernel, q_scale=q_scale),
        grid=(t // tm,),
        in_specs=[row, pl.BlockSpec((1, d), lambda i: (0, 0)), mod, mod,
                  _resident(w3.shape), anyspec, anyspec],
        out_specs=[pl.BlockSpec((tm, (ng - N_QKV) * d), lambda i: (i, 0)),
                   pl.BlockSpec((tm, N_QKV * d), lambda i: (i, 0)),
                   stack, stack],
        out_shape=[jax.ShapeDtypeStruct((t, (ng - N_QKV) * d), F32),
                   jax.ShapeDtypeStruct((t, N_QKV * d), BF16),
                   jax.ShapeDtypeStruct(kst.shape, F32),
                   jax.ShapeDtypeStruct(vst.shape, F32)],
        scratch_shapes=[pltpu.VMEM((tm, d), BF16)],
        input_output_aliases={5: 2, 6: 3},
        compiler_params=_params("parallel"),
        name="inproj",
    )(x, g, sh, sc, w3, kst, vst)


def _lambda(lamp_ref, lam_init):
    lp = lamp_ref[...]
    s01 = jnp.sum(lp[0:1] * lp[1:2], axis=-1, keepdims=True)
    s23 = jnp.sum(lp[2:3] * lp[3:4], axis=-1, keepdims=True)
    return jnp.exp(s01) - jnp.exp(s23) + lam_init


def _split_maps(q, hd):
    lane = lax.broadcasted_iota(jnp.int32, q.shape, 1)
    zero = jnp.zeros_like(q)
    return jnp.concatenate([jnp.where(lane < hd, q, zero), jnp.where(lane >= hd, q, zero)],
                           axis=0).astype(BF16)


def _visible(rows, cols, tq, q0, k0):
    r = lax.broadcasted_iota(jnp.int32, (rows, cols), 0)
    qpos = q0 + jnp.where(r >= tq, r - tq, r)
    kpos = k0 + lax.broadcasted_iota(jnp.int32, (rows, cols), 1)
    return kpos < (qpos // CHUNK + 1) * CHUNK


def _attn_finish(o1, o2, lam, lam_init, gsub):
    o = o1 - lam * o2
    y = o * lax.rsqrt(jnp.mean(o * o, axis=-1, keepdims=True) + EPS) * gsub
    return (y * (1.0 - lam_init)).astype(BF16)


def _attn_kernel(cst_ref, lamp_ref, gsub_ref, q_ref, k_ref, v_ref, o_ref,
                 qs_scr, vx_scr, m_scr, acc_scr, *, tq, kw, hd):
    i = pl.program_id(2)
    lam_init = cst_ref[0]

    @pl.when(i == 0)
    def _():
        vx_scr[:, :LANES] = v_ref[...]
        vx_scr[:, LANES:] = jnp.ones((vx_scr.shape[0], LANES), BF16)

    q = q_ref[...]
    lane = lax.broadcasted_iota(jnp.int32, q.shape, 1)
    qs_scr[:tq] = jnp.where(lane < hd, q, jnp.zeros_like(q))
    qs_scr[tq:] = jnp.where(lane >= hd, q, jnp.zeros_like(q))
    def kv_step(start, width, masked, first=False):
        k = k_ref[pl.ds(start, width), :]
        vx = vx_scr[pl.ds(start, width), :]
        if masked:
            vis = _visible(tq, width, tq, i * tq, start)
        maps = (slice(0, tq), slice(tq, 2 * tq))
        scores = [lax.dot_general(qs_scr[rows], k, _NT, preferred_element_type=F32)
                  for rows in maps]
        for rows, s in zip(maps, scores):
            if masked:
                s = jnp.where(vis, s, MASK_VALUE)
            m_cur = jnp.max(s, axis=-1, keepdims=True)
            m_new = jnp.broadcast_to(m_cur, (tq, LANES)) if first else jnp.maximum(m_scr[rows], m_cur)
            p = jnp.exp2(s - jnp.tile(m_new, (1, width // LANES))).astype(BF16)
            pv = jnp.dot(p, vx, preferred_element_type=F32)
            if first:
                acc_scr[rows] = pv
            else:
                alpha = jnp.exp2(m_scr[rows] - m_new)
                acc_scr[rows] = jnp.tile(alpha, (1, 2)) * acc_scr[rows] + pv
            m_scr[rows] = m_new

    for pieces in range(kw // tq):
        @pl.when(i % (kw // tq) == pieces)
        def _():
            kv_step(pl.multiple_of((i - pieces) * tq, tq), (pieces + 1) * tq, True, first=True)

    def full_step(j):
        kv_step(pl.multiple_of(j * kw, kw), kw, False)

    def full_body(k, carry):
        for s in range(KV_STEPS_PER_TRIP):
            full_step(KV_STEPS_PER_TRIP * k + s)
        return carry

    n_full = (i * tq) // kw
    lax.fori_loop(0, n_full // KV_STEPS_PER_TRIP, full_body, 0)

    for left in range(1, KV_STEPS_PER_TRIP):
        @pl.when(n_full % KV_STEPS_PER_TRIP == left)
        def _():
            for s in range(left):
                full_step(n_full - left + s)

    acc = acc_scr[...]
    o12 = acc[:, :LANES] / acc[:, LANES:]
    lam = _lambda(lamp_ref, lam_init)
    o_ref[...] = _attn_finish(o12[:tq], o12[tq:], lam, lam_init, gsub_ref[...])


def _attn(cst, lamp, gsub, qkv, layer, *, nb, t, nh, tq):
    del layer
    bt, d = qkv.shape[0], nh * LANES
    nq = t // tq
    kw = KV_TILE if KV_TILE % tq == 0 and KV_TILE <= t else tq

    def kv(group):
        return pl.BlockSpec((t, LANES), lambda b, h, i: (b, group * nh + h))

    return pl.pallas_call(
        functools.partial(_attn_kernel, tq=tq, kw=kw, hd=LANES // 2),
        grid=(nb, nh, nq),
        in_specs=[pl.BlockSpec(memory_space=pltpu.SMEM),
                  pl.BlockSpec(lamp.shape, lambda b, h, i: (0, 0)),
                  pl.BlockSpec((1, LANES), lambda b, h, i: (0, 0)),
                  pl.BlockSpec((tq, LANES), lambda b, h, i: (b * nq + i, h)),
                  kv(1), kv(2)],
        out_specs=pl.BlockSpec((tq, LANES), lambda b, h, i: (b * nq + i, h)),
        out_shape=jax.ShapeDtypeStruct((bt, d), BF16),
        scratch_shapes=[pltpu.VMEM((2 * tq, LANES), BF16), pltpu.VMEM((t, 2 * LANES), BF16),
                        pltpu.VMEM((2 * tq, LANES), F32), pltpu.VMEM((2 * tq, 2 * LANES), F32)],
        compiler_params=_params("parallel", "parallel", "arbitrary"),
        name="attn",
    )(cst, lamp, gsub, qkv, qkv, qkv)


def _attn_cached_kernel(cst_ref, lamp_ref, gsub_ref, q_ref, kc_ref, vc_ref, kn_ref, vn_ref, o_ref,
                        *, tq, past, hd):
    lam_init = cst_ref[0]
    lam = _lambda(lamp_ref, lam_init)
    nh = q_ref.shape[1] // LANES
    masked = past % CHUNK + tq > CHUNK
    for h in range(nh):
        cols = slice(h * LANES, (h + 1) * LANES)
        head = pl.ds(h, past, stride=nh)
        qs = _split_maps(q_ref[:, cols], hd)

        def scores(k, k0):
            s = lax.dot_general(qs, k.astype(BF16), _NT, preferred_element_type=F32)
            if masked:
                s = jnp.where(_visible(2 * tq, k.shape[0], tq, past, k0), s, MASK_VALUE)
            return s

        sc = scores(kc_ref[head, :], 0)
        sn = scores(kn_ref[:, cols], past)
        m = jnp.maximum(jnp.max(sc, axis=-1, keepdims=True), jnp.max(sn, axis=-1, keepdims=True))
        pc = jnp.exp2(sc - m)
        pn = jnp.exp2(sn - m)
        l = jnp.sum(pc, axis=-1, keepdims=True) + jnp.sum(pn, axis=-1, keepdims=True)
        acc = (jnp.dot(pc.astype(BF16), vc_ref[head, :].astype(BF16), preferred_element_type=F32)
               + jnp.dot(pn.astype(BF16), vn_ref[:, cols], preferred_element_type=F32))
        o12 = acc / l
        o_ref[:, cols] = _attn_finish(o12[:tq], o12[tq:], lam, lam_init, gsub_ref[...])


def _attn_cached(cst, lamp, gsub, qkv, ck, cv, layer, *, nb, t, nh):
    bt, d = qkv.shape[0], nh * LANES
    past = ck.shape[2]
    ck, cv = (c.reshape(c.shape[0], nb, past * nh, LANES) for c in (ck, cv))
    cache = pl.BlockSpec((None, None, past * nh, LANES), lambda b: (layer, b, 0, 0))

    def rows(group):
        return pl.BlockSpec((t, d), lambda b: (b, group))

    return pl.pallas_call(
        functools.partial(_attn_cached_kernel, tq=t, past=past, hd=LANES // 2),
        grid=(nb,),
        in_specs=[pl.BlockSpec(memory_space=pltpu.SMEM),
                  pl.BlockSpec(lamp.shape, lambda b: (0, 0)),
                  pl.BlockSpec((1, LANES), lambda b: (0, 0)),
                  rows(0), cache, cache, rows(1), rows(2)],
        out_specs=pl.BlockSpec((t, d), lambda b: (b, 0)),
        out_shape=jax.ShapeDtypeStruct((bt, d), BF16),
        compiler_params=_params("parallel"),
        name="attn_cached",
    )(cst, lamp, gsub, qkv, ck, cv, qkv, qkv)


def _split3(x):
    hi = x.astype(BF16)
    r = x - hi.astype(F32)
    mid = r.astype(BF16)
    lo = (r - mid.astype(F32)).astype(BF16)
    return hi, mid, lo


def _hgrn_kernel(*refs, heads, tt, blk, has_s0):
    n_in = 10 if has_s0 else 9
    for h in range(heads):
        cols = pl.ds(h * LANES, LANES)
        per_head = [r.at[:, cols] for r in refs[:5]] + list(refs[5:9])
        if has_s0:
            per_head.append(refs[9].at[h])
        o_ref, sf_ref, st_scr, pad_scr = refs[n_in:]
        _hgrn_head(*per_head, o_ref.at[:, cols], sf_ref.at[h], st_scr.at[h], pad_scr.at[h],
                   tt=tt, blk=blk, has_s0=has_s0)


def _hgrn_head(*refs, tt, blk, has_s0):
    hq_ref, hf_ref, hi_ref, hg_ref, lb_ref, gn_ref, tri_ref, ones2_ref, lvl_ref = refs[:9]
    s0_ref = refs[9] if has_s0 else None
    o_ref, sf_ref, st_scr, pad_scr = refs[-4:]
    t = pl.program_id(2)

    @pl.when(t == 0)
    def _():
        if has_s0:
            st_scr[...] = s0_ref[...].T
        else:
            st_scr[...] = jnp.zeros_like(st_scr)
        pad_scr[:, 0:PAD, :] = jnp.zeros((3, PAD, LANES), F32)

    lb = lb_ref[...]
    sig = jax.nn.sigmoid(hf_ref[...])
    logf = jnp.log(jnp.maximum(lb + (1.0 - lb) * sig, TINY)) * math.log2(math.e)
    kk = (1.0 - lb) * (1.0 - sig)
    qq = _silu(hq_ref[...])
    vv = hi_ref[...]

    tri = tri_ref[...]
    parts = _split3(logf)
    b = jnp.concatenate(
        [sum(jnp.dot(tri, part[c * blk:(c + 1) * blk], preferred_element_type=F32)
             for part in parts) for c in range(tt // blk)], axis=0)

    pad_scr[0, PAD:, :] = kk
    pad_scr[1, PAD:, :] = b
    pad_scr[2, PAD:, :] = vv
    ones2 = ones2_ref[...]
    pos = lax.broadcasted_iota(jnp.int32, (PAD, LANES), 0) % SUB

    def lag_terms(d):
        if d == 0:
            return (qq * kk).astype(BF16), vv
        ks, bs, vs = (pad_scr[r, PAD - d:PAD - d + tt, :] for r in range(3))
        off = jnp.tile(jnp.where(pos >= d, 0.0, MASK_VALUE), (tt // PAD, 1))
        return (qq * ks * jnp.exp2(b - bs + off)).astype(BF16), vs

    o = jnp.zeros((tt, LANES), F32)
    for d in range(0, SUB, 2):
        (p0, v0), (p1, v1) = lag_terms(d), lag_terms(d + 1)
        r = jnp.dot(jnp.concatenate([p0, p1], axis=1), ones2, preferred_element_type=F32)
        o = o + r[:, :LANES] * v0 + r[:, LANES:] * v1

    st = st_scr[...]
    older = []
    for c in range(tt // blk):
        sl = slice(c * blk, (c + 1) * blk)
        bc = b[sl]
        b_last = bc[blk - 1:blk]
        qt = (qq[sl] * jnp.exp2(bc)).astype(BF16)
        older.append(lax.dot_general(qt, st.astype(BF16), _NT, preferred_element_type=F32))
        kt = (kk[sl] * jnp.exp2(b_last - bc)).astype(BF16)
        st = st * jnp.exp2(b_last) + lax.dot_general(vv[sl].astype(BF16), kt, _TN,
                                                    preferred_element_type=F32)
    st_scr[...] = st
    o = o + jnp.concatenate(older, axis=0)

    weights = [None] * (tt // blk)
    m, level = SUB, 0
    while 2 * m <= blk:
        b3 = b.reshape(tt // (2 * m), 2 * m, LANES)
        w = jnp.exp2(-jnp.abs(b3 - b3[:, m - 1:m, :])).reshape(tt, LANES)
        qt = (qq * w).astype(BF16)
        kt = (kk * w).astype(BF16)
        for c in range(tt // blk):
            sl = slice(c * blk, (c + 1) * blk)
            a = lax.dot_general(qt[sl], kt[sl], _NT, preferred_element_type=F32) * lvl_ref[level]
            weights[c] = a if weights[c] is None else weights[c] + a
        m, level = 2 * m, level + 1
    if level:
        o = o + jnp.concatenate(
            [jnp.dot(weights[c].astype(BF16), vv[c * blk:(c + 1) * blk].astype(BF16),
                     preferred_element_type=F32) for c in range(tt // blk)], axis=0)

    y = o * lax.rsqrt(jnp.mean(o * o, axis=-1, keepdims=True) + EPS) * gn_ref[...]
    o_ref[...] = (y * _silu(hg_ref[...])).astype(BF16)

    @pl.when(t == pl.num_programs(2) - 1)
    def _():
        sf_ref[...] = st.T


def _hgrn(z, lb, gn, s0, layer, *, nb, t, nh, tt, blk, heads):
    bt = z.shape[0]
    nt = t // tt
    has_s0 = s0 is not None
    ng = nh // heads
    wide = heads * LANES

    def col(group):
        return pl.BlockSpec((tt, wide), lambda b, h, i: (b * nt + i, group * ng + h))

    state = pl.BlockSpec((None, heads, LANES, LANES), lambda b, h, i: (b, h, 0, 0))
    tri = jnp.tril(jnp.ones((blk, blk), BF16))
    ones2 = jnp.kron(jnp.eye(2, dtype=BF16), jnp.ones((LANES, LANES), BF16))
    pos = jnp.arange(blk)
    sizes = [SUB << i for i in range(max(blk // SUB, 1).bit_length() - 1)]
    lvl = jnp.stack([((pos[:, None] // (2 * m) == pos[None, :] // (2 * m))
                      & (pos[:, None] % (2 * m) >= m) & (pos[None, :] % (2 * m) < m))
                     for m in sizes]).astype(F32) if sizes else jnp.zeros((1, blk, blk), F32)
    in_specs = [col(0), col(1), col(2), col(3),
                pl.BlockSpec((1, wide), lambda b, h, i: (0, h)),
                pl.BlockSpec((1, LANES), lambda b, h, i: (0, 0)),
                _resident(tri.shape), _resident(ones2.shape), _resident(lvl.shape)]
    args = [z, z, z, z, lb, gn, tri, ones2, lvl]
    if has_s0:
        in_specs.append(pl.BlockSpec((None, None, heads, LANES, LANES),
                                     lambda b, h, i: (layer, b, h, 0, 0)))
        args.append(s0)
    return pl.pallas_call(
        functools.partial(_hgrn_kernel, heads=heads, tt=tt, blk=blk, has_s0=has_s0),
        grid=(nb, ng, nt),
        in_specs=in_specs,
        out_specs=[pl.BlockSpec((tt, wide), lambda b, h, i: (b * nt + i, h)), state],
        out_shape=[jax.ShapeDtypeStruct((bt, nh * LANES), BF16),
                   jax.ShapeDtypeStruct((nb, nh, LANES, LANES), F32)],
        scratch_shapes=[pltpu.VMEM((heads, LANES, LANES), F32),
                        pltpu.VMEM((heads, 3, PAD + tt, LANES), F32)],
        compiler_params=_params("parallel", "parallel", "arbitrary"),
        name="hgrn",
    )(*args)


def _merge_kernel(x_ref, gt_ref, oa_ref, oh_ref, za_ref, zh_ref, wa_ref, wh_ref, wo_ref, o_ref):
    a = jnp.dot(oa_ref[...], wa_ref[...], preferred_element_type=F32)
    r = jnp.dot(oh_ref[...], wh_ref[...], preferred_element_type=F32)
    merged = jax.nn.sigmoid(za_ref[...]) * a + jax.nn.sigmoid(zh_ref[...]) * r
    o_ref[...] = x_ref[...] + gt_ref[...] * jnp.dot(merged.astype(BF16), wo_ref[...],
                                                    preferred_element_type=F32)


def _merge(x, gt, oa, oh, z, wa, wh, wo, *, tm, per_row, rows_per_seq):
    t, d = x.shape
    mod = _mod_spec(per_row, tm, d, rows_per_seq)
    row = pl.BlockSpec((tm, d), lambda i: (i, 0))
    return pl.pallas_call(
        _merge_kernel,
        grid=(t // tm,),
        in_specs=[row, mod, row, row,
                  pl.BlockSpec((tm, d), lambda i: (i, 4)), pl.BlockSpec((tm, d), lambda i: (i, 5)),
                  _resident(wa.shape), _resident(wh.shape), _resident(wo.shape)],
        out_specs=row,
        out_shape=jax.ShapeDtypeStruct((t, d), F32),
        compiler_params=_params("parallel"),
        name="merge",
    )(x, gt, oa, oh, z, z, wa, wh, wo)


def _tile(n, target):
    if n <= target:
        return n
    for c in range(target, 7, -1):
        if n % c == 0 and c % 8 == 0:
            return c
    return n


def _layer_weights(p, l, d, fc):
    f = p["w_ffn1_d"].shape[1]
    nc = f // fc

    def ffn(w_gu, w_d):
        wg = w_gu[l, :, :f].reshape(d, nc, fc).transpose(1, 0, 2).astype(BF16)
        wu = w_gu[l, :, f:].reshape(d, nc, fc).transpose(1, 0, 2).astype(BF16)
        return wg, wu, w_d[l].reshape(nc, fc, d).astype(BF16)

    ng = p["w_in"].shape[2] // d
    return dict(
        ffn1=ffn(p["w_ffn1_gu"], p["w_ffn1_d"]),
        ffn2=ffn(p["w_ffn2_gu"], p["w_ffn2_d"]),
        w_in=p["w_in"][l].reshape(d, ng, d).transpose(1, 0, 2).astype(BF16),
        w_ba=p["w_br_att"][l].astype(BF16),
        w_bh=p["w_br_hg"][l].astype(BF16),
        w_out=p["w_out"][l].astype(BF16),
    )


def _trunk(x3, mods, p, lb_all, weights, cache):
    nb, t, d = x3.shape
    depth = p["w_ada"].shape[0]
    nh = d // LANES
    bt = nb * t
    x = x3.reshape(bt, d)
    per_row = t < 256
    tm = _tile(bt if per_row else t, 512)
    kst = jnp.zeros((depth, bt * nh, LANES), F32)
    vst = jnp.zeros((depth, bt * nh, LANES), F32)
    states = []
    for l in range(depth):
        w = weights[l]
        m = mods[l]
        if per_row:
            mm = [jnp.repeat(m[:, i], t, axis=0) for i in range(N_ADA)]
        else:
            mm = [m[:, i].reshape(nb, 1, d) for i in range(N_ADA)]
        sh1, sc1, g1, sh2, sc2, g2, sh3, sc3, g3 = mm
        kw = dict(tm=tm, per_row=per_row, rows_per_seq=t)
        row = lambda a: a.reshape(1, -1)

        g_final = row(p["g_final"])
        x = _ffn(x, row(p["g_ffn1"][l]), sh1, sc1, g1, *w["ffn1"], g_final, final_norm=False, **kw)
        q_scale = (LANES // 2) ** -0.5 * math.log2(math.e)
        kw_in = dict(kw, tm=_tile(tm, 256))
        z, qkv, kst, vst = _inproj(x, row(p["g_mix"][l]), sh2, sc2, w["w_in"], kst, vst, l,
                                   q_scale=q_scale, **kw_in)

        lam_init = 0.8 - 0.6 * math.exp(-0.3 * l)
        cst = jnp.array([lam_init], F32)
        gsub = row(p["g_att_sub"][l])
        if cache is None:
            oa = _attn(cst, p["att_lambda"][l], gsub, qkv, l, nb=nb, t=t, nh=nh,
                       tq=_tile(t, Q_TILE))
            s0 = None
        else:
            ck, cv, s0 = cache
            oa = _attn_cached(cst, p["att_lambda"][l], gsub, qkv, ck, cv, l, nb=nb, t=t, nh=nh)
        tt = _tile(t, 2048)
        blk = min(tt, 128)
        oh, s_fin = _hgrn(z, row(lb_all[l]), row(p["g_hg_norm"][l]), s0, l,
                          nb=nb, t=t, nh=nh, tt=tt, blk=blk, heads=nh if per_row else 1)
        states.append(s_fin)

        x = _merge(x, g2, oa, oh, z, w["w_ba"], w["w_bh"], w["w_out"], **kw)
        x = _ffn(x, row(p["g_ffn2"][l]), sh3, sc3, g3, *w["ffn2"], g_final,
                 final_norm=l == depth - 1, **kw)

    y = x.reshape(nb, t, d)
    shape5 = (depth, nb, t, nh, LANES)
    return y, kst.reshape(shape5), vst.reshape(shape5), jnp.stack(states)


def kernel(x_prompt, x_sample, cache_k, cache_v, state_hgrn, c_prompt, c_sample, w_ada, b_ada, g_ffn1, w_ffn1_gu, w_ffn1_d, g_mix, w_in, att_lambda, g_att_sub, hg_lb_logits, g_hg_norm, w_br_att, w_br_hg, w_out, g_ffn2, w_ffn2_gu, w_ffn2_d, g_final):
    p = dict(w_ada=w_ada, b_ada=b_ada, g_ffn1=g_ffn1, w_ffn1_gu=w_ffn1_gu, w_ffn1_d=w_ffn1_d,
             g_mix=g_mix, w_in=w_in, att_lambda=att_lambda, g_att_sub=g_att_sub,
             g_hg_norm=g_hg_norm, w_br_att=w_br_att, w_br_hg=w_br_hg, w_out=w_out,
             g_ffn2=g_ffn2, w_ffn2_gu=w_ffn2_gu, w_ffn2_d=w_ffn2_d, g_final=g_final)
    depth, d, _ = w_ada.shape
    nbp, nbs = x_prompt.shape[0], x_sample.shape[0]

    lb_sm = jax.nn.softmax(hg_lb_logits.astype(F32), axis=0)
    lb_all = jnp.cumsum(lb_sm, axis=0) - lb_sm[0]

    ns = nbp + nbs
    nsp = -(-ns // 8) * 8
    c_all = jnp.concatenate([c_prompt, c_sample, jnp.zeros((nsp - ns, d), F32)], axis=0)
    mods = _ada(c_all, w_ada, b_ada).reshape(depth, nsp, N_ADA, d)

    f = w_ffn1_d.shape[1]
    fc = 256 if f % 256 == 0 else LANES
    weights = [_layer_weights(p, l, d, fc) for l in range(depth)]

    y_p, k_p, v_p, s_p = _trunk(x_prompt, mods[:, :nbp], p, lb_all, weights, None)
    y_s, k_s, v_s, s_s = _trunk(x_sample, mods[:, nbp:ns], p, lb_all, weights,
                                (cache_k, cache_v, state_hgrn))
    return (y_p, y_s, k_p, v_p, s_p, k_s, v_s, s_s)
```

```python
import functools
import math

import jax
import jax.numpy as jnp
from jax import lax
from jax.experimental import pallas as pl
from jax.experimental.pallas import tpu as pltpu

F32 = jnp.float32
BF16 = jnp.bfloat16

CHUNK = 64
N_ADA = 9
MACARON_WEIGHT = 0.5
EPS = 1e-6
MASK_VALUE = -1e30
TINY = 1e-30
LANES = 128
SUB = 4
PAD = 8
VMEM_LIMIT = 56 * 1024 * 1024
Q_TILE = 512
KV_TILE = 2048
KV_STEPS_PER_TRIP = 2

_NT = (((1,), (1,)), ((), ()))
_TN = (((0,), (0,)), ((), ()))


def _params(*sem):
    return pltpu.CompilerParams(dimension_semantics=sem, vmem_limit_bytes=VMEM_LIMIT)


def _resident(shape):
    nd = len(shape)
    return pl.BlockSpec(shape, lambda *_: (0,) * nd, pipeline_mode=pl.Buffered(1))


def _modnorm(x, g, sc, sh):
    y = x * lax.rsqrt(jnp.mean(x * x, axis=-1, keepdims=True) + EPS) * g
    return y * (1.0 + sc) + sh


def _silu(x):
    return x * jax.nn.sigmoid(x)


def _mod_spec(per_row, tm, d, rows_per_seq):
    if per_row:
        return pl.BlockSpec((tm, d), lambda i, *_: (i, 0))
    tiles = rows_per_seq // tm
    return pl.BlockSpec((None, 1, d), lambda i, *_: (i // tiles, 0, 0))


def _ada_kernel(c_ref, w_ref, b_ref, o_ref):
    c = c_ref[...]
    o_ref[...] = jnp.dot(_silu(c).astype(BF16), w_ref[...].astype(BF16),
                         preferred_element_type=F32) + b_ref[...]


def _ada(c_all, w_ada, b_ada):
    depth, d, _ = w_ada.shape
    ns = c_all.shape[0]
    wide = 3 * d
    return pl.pallas_call(
        _ada_kernel,
        grid=(depth, N_ADA * d // wide),
        in_specs=[pl.BlockSpec((ns, d), lambda l, j: (0, 0)),
                  pl.BlockSpec((None, d, wide), lambda l, j: (l, 0, j)),
                  pl.BlockSpec((None, 1, wide), lambda l, j: (l, 0, j))],
        out_specs=pl.BlockSpec((None, ns, wide), lambda l, j: (l, 0, j)),
        out_shape=jax.ShapeDtypeStruct((depth, ns, N_ADA * d), F32),
        compiler_params=_params("parallel", "parallel"),
        name="ada",
    )(c_all, w_ada, b_ada.reshape(depth, 1, N_ADA * d))


def _ffn_kernel(x_ref, g_ref, sh_ref, sc_ref, gt_ref, wg_ref, wu_ref, wd_ref, gf_ref, o_ref,
                h_scr, acc_scr, *, n_chunks, final_norm):
    h_scr[...] = _modnorm(x_ref[...], g_ref[...], sc_ref[...], sh_ref[...]).astype(BF16)
    for c in range(n_chunks):
        h = h_scr[...]
        gate = jnp.dot(h, wg_ref[c], preferred_element_type=F32)
        up = jnp.dot(h, wu_ref[c], preferred_element_type=F32)
        a = (_silu(gate) * up).astype(BF16)
        down = jnp.dot(a, wd_ref[c], preferred_element_type=F32)
        if c == 0:
            acc_scr[...] = down
        else:
            acc_scr[...] += down
    y = x_ref[...] + MACARON_WEIGHT * gt_ref[...] * acc_scr[...]
    if final_norm:
        y = y * lax.rsqrt(jnp.mean(y * y, axis=-1, keepdims=True) + EPS) * gf_ref[...]
    o_ref[...] = y


def _ffn(x, g, sh, sc, gt, wg, wu, wd, g_final, *, tm, per_row, rows_per_seq, final_norm):
    t, d = x.shape
    n_chunks = wg.shape[0]
    mod = _mod_spec(per_row, tm, d, rows_per_seq)
    row = pl.BlockSpec((tm, d), lambda i: (i, 0))
    gain = pl.BlockSpec((1, d), lambda i: (0, 0))
    return pl.pallas_call(
        functools.partial(_ffn_kernel, n_chunks=n_chunks, final_norm=final_norm),
        grid=(t // tm,),
        in_specs=[row, gain, mod, mod, mod,
                  _resident(wg.shape), _resident(wu.shape), _resident(wd.shape), gain],
        out_specs=row,
        out_shape=jax.ShapeDtypeStruct((t, d), F32),
        scratch_shapes=[pltpu.VMEM((tm, d), BF16), pltpu.VMEM((tm, d), F32)],
        compiler_params=_params("parallel"),
        name="ffn",
    )(x, g, sh, sc, gt, wg, wu, wd, g_final)


N_QKV = 3


def _inproj_kernel(x_ref, g_ref, sh_ref, sc_ref, w_ref, kin_ref, vin_ref,
                   z_ref, qkv_ref, k_ref, v_ref, u_scr, *, q_scale):
    del kin_ref, vin_ref
    d = u_scr.shape[1]

    def group(g):
        return jnp.dot(u_scr[...], w_ref[g], preferred_element_type=F32)

    def cache_rows(ref, z):
        tm, nh = z.shape[0], z.shape[1] // LANES
        for h in range(nh):
            ref[pl.ds(h, tm, stride=nh), :] = z[:, h * LANES:(h + 1) * LANES]

    u_scr[...] = _modnorm(x_ref[...], g_ref[...], sc_ref[...], sh_ref[...]).astype(BF16)
    qkv_ref[:, 0:d] = (group(0) * q_scale).astype(BF16)
    for g, ref in ((1, k_ref), (2, v_ref)):
        z = group(g)
        cache_rows(ref, z)
        qkv_ref[:, g * d:(g + 1) * d] = z.astype(BF16)
    for g in range(N_QKV, w_ref.shape[0]):
        z_ref[:, (g - N_QKV) * d:(g - N_QKV + 1) * d] = group(g)


def _inproj(x, g, sh, sc, w3, kst, vst, layer, *, tm, per_row, rows_per_seq, q_scale):
    t, d = x.shape
    ng = w3.shape[0]
    mod = _mod_spec(per_row, tm, d, rows_per_seq)
    row = pl.BlockSpec((tm, d), lambda i: (i, 0))
    stack = pl.BlockSpec((None, tm * (d // LANES), LANES), lambda i: (layer, i, 0))
    anyspec = pl.BlockSpec(memory_space=pl.ANY)
    return pl.pallas_call(
        functools.partial(_inproj_kernel, q_scale=q_scale),
        grid=(t // tm,),
        in_specs=[row, pl.BlockSpec((1, d), lambda i: (0, 0)), mod, mod,
                  _resident(w3.shape), anyspec, anyspec],
        out_specs=[pl.BlockSpec((tm, (ng - N_QKV) * d), lambda i: (i, 0)),
                   pl.BlockSpec((tm, N_QKV * d), lambda i: (i, 0)),
                   stack, stack],
        out_shape=[jax.ShapeDtypeStruct((t, (ng - N_QKV) * d), F32),
                   jax.ShapeDtypeStruct((t, N_QKV * d), BF16),
                   jax.ShapeDtypeStruct(kst.shape, F32),
                   jax.ShapeDtypeStruct(vst.shape, F32)],
        scratch_shapes=[pltpu.VMEM((tm, d), BF16)],
        input_output_aliases={5: 2, 6: 3},
        compiler_params=_params("parallel"),
        name="inproj",
    )(x, g, sh, sc, w3, kst, vst)


def _lambda(lamp_ref, lam_init):
    lp = lamp_ref[...]
    s01 = jnp.sum(lp[0:1] * lp[1:2], axis=-1, keepdims=True)
    s23 = jnp.sum(lp[2:3] * lp[3:4], axis=-1, keepdims=True)
    return jnp.exp(s01) - jnp.exp(s23) + lam_init


def _split_maps(q, hd):
    lane = lax.broadcasted_iota(jnp.int32, q.shape, 1)
    zero = jnp.zeros_like(q)
    return jnp.concatenate([jnp.where(lane < hd, q, zero), jnp.where(lane >= hd, q, zero)],
                           axis=0).astype(BF16)


def _visible(rows, cols, tq, q0, k0):
    r = lax.broadcasted_iota(jnp.int32, (rows, cols), 0)
    qpos = q0 + jnp.where(r >= tq, r - tq, r)
    kpos = k0 + lax.broadcasted_iota(jnp.int32, (rows, cols), 1)
    return kpos < (qpos // CHUNK + 1) * CHUNK


def _attn_finish(o1, o2, lam, lam_init, gsub):
    o = o1 - lam * o2
    y = o * lax.rsqrt(jnp.mean(o * o, axis=-1, keepdims=True) + EPS) * gsub
    return (y * (1.0 - lam_init)).astype(BF16)


def _attn_kernel(cst_ref, lamp_ref, gsub_ref, bias_ref, q_ref, k_ref, v_ref, o_ref,
                 qs_scr, vx_scr, m_scr, acc_scr, *, tq, kw, hd):
    i = pl.program_id(2)
    lam_init = cst_ref[0]

    @pl.when(i == 0)
    def _():
        vx_scr[:, :LANES] = v_ref[...]
        vx_scr[:, LANES:] = jnp.ones((vx_scr.shape[0], LANES), BF16)

    q = q_ref[...]
    lane = lax.broadcasted_iota(jnp.int32, q.shape, 1)
    qs_scr[:tq] = jnp.where(lane < hd, q, jnp.zeros_like(q))
    qs_scr[tq:] = jnp.where(lane >= hd, q, jnp.zeros_like(q))
    def kv_step(start, width, masked, first=False):
        k = k_ref[pl.ds(start, width), :]
        vx = vx_scr[pl.ds(start, width), :]
        maps = (slice(0, tq), slice(tq, 2 * tq))
        scores = [lax.dot_general(qs_scr[rows], k, _NT, preferred_element_type=F32)
                  for rows in maps]
        for rows, s in zip(maps, scores):
            if masked:
                diag = s[:, width - tq:] + bias_ref[...]
                s = diag if width == tq else jnp.concatenate([s[:, :width - tq], diag], axis=1)
            m_cur = jnp.max(s, axis=-1, keepdims=True)
            m_new = jnp.broadcast_to(m_cur, (tq, LANES)) if first else jnp.maximum(m_scr[rows], m_cur)
            p = jnp.exp2(s - jnp.tile(m_new, (1, width // LANES))).astype(BF16)
            pv = jnp.dot(p, vx, preferred_element_type=F32)
            if first:
                acc_scr[rows] = pv
            else:
                alpha = jnp.exp2(m_scr[rows] - m_new)
                acc_scr[rows] = jnp.tile(alpha, (1, 2)) * acc_scr[rows] + pv
            m_scr[rows] = m_new

    def full_step(j):
        kv_step(pl.multiple_of(j * kw, kw), kw, False)

    n_full = (i * tq) // kw

    for pieces in range(kw // tq):
        for left in range(KV_STEPS_PER_TRIP):
            @pl.when((i % (kw // tq) == pieces) & (n_full % KV_STEPS_PER_TRIP == left))
            def _():
                kv_step(pl.multiple_of((i - pieces) * tq, tq), (pieces + 1) * tq, True, first=True)
                for s in range(left):
                    full_step(n_full - left + s)

    def full_body(k, carry):
        for s in range(KV_STEPS_PER_TRIP):
            full_step(KV_STEPS_PER_TRIP * k + s)
        return carry

    lax.fori_loop(0, n_full // KV_STEPS_PER_TRIP, full_body, 0)

    acc = acc_scr[...]
    o12 = acc[:, :LANES] / acc[:, LANES:]
    lam = _lambda(lamp_ref, lam_init)
    o_ref[...] = _attn_finish(o12[:tq], o12[tq:], lam, lam_init, gsub_ref[...])


def _attn(cst, lamp, gsub, qkv, layer, *, nb, t, nh, tq):
    del layer
    bt, d = qkv.shape[0], nh * LANES
    nq = t // tq
    kw = KV_TILE if KV_TILE % tq == 0 and KV_TILE <= t else tq
    assert tq % CHUNK == 0 or nq == 1, (tq, CHUNK)
    chunk = jnp.arange(tq) // CHUNK
    bias = jnp.where(chunk[None, :] <= chunk[:, None], 0.0, MASK_VALUE).astype(F32)

    def kv(group):
        return pl.BlockSpec((t, LANES), lambda b, h, i: (b, group * nh + h))

    return pl.pallas_call(
        functools.partial(_attn_kernel, tq=tq, kw=kw, hd=LANES // 2),
        grid=(nb, nh, nq),
        in_specs=[pl.BlockSpec(memory_space=pltpu.SMEM),
                  pl.BlockSpec(lamp.shape, lambda b, h, i: (0, 0)),
                  pl.BlockSpec((1, LANES), lambda b, h, i: (0, 0)),
                  _resident(bias.shape),
                  pl.BlockSpec((tq, LANES), lambda b, h, i: (b * nq + i, h)),
                  kv(1), kv(2)],
        out_specs=pl.BlockSpec((tq, LANES), lambda b, h, i: (b * nq + i, h)),
        out_shape=jax.ShapeDtypeStruct((bt, d), BF16),
        scratch_shapes=[pltpu.VMEM((2 * tq, LANES), BF16), pltpu.VMEM((t, 2 * LANES), BF16),
                        pltpu.VMEM((2 * tq, LANES), F32), pltpu.VMEM((2 * tq, 2 * LANES), F32)],
        compiler_params=_params("parallel", "parallel", "arbitrary"),
        name="attn",
    )(cst, lamp, gsub, bias, qkv, qkv, qkv)


def _attn_cached_kernel(cst_ref, lamp_ref, gsub_ref, q_ref, kc_ref, vc_ref, kn_ref, vn_ref, o_ref,
                        *, tq, past, hd):
    lam_init = cst_ref[0]
    lam = _lambda(lamp_ref, lam_init)
    nh = q_ref.shape[1] // LANES
    masked = past % CHUNK + tq > CHUNK
    for h in range(nh):
        cols = slice(h * LANES, (h + 1) * LANES)
        head = pl.ds(h, past, stride=nh)
        qs = _split_maps(q_ref[:, cols], hd)

        def scores(k, k0):
            s = lax.dot_general(qs, k.astype(BF16), _NT, preferred_element_type=F32)
            if masked:
                s = jnp.where(_visible(2 * tq, k.shape[0], tq, past, k0), s, MASK_VALUE)
            return s

        sc = scores(kc_ref[head, :], 0)
        sn = scores(kn_ref[:, cols], past)
        m = jnp.maximum(jnp.max(sc, axis=-1, keepdims=True), jnp.max(sn, axis=-1, keepdims=True))
        pc = jnp.exp2(sc - m)
        pn = jnp.exp2(sn - m)
        l = jnp.sum(pc, axis=-1, keepdims=True) + jnp.sum(pn, axis=-1, keepdims=True)
        acc = (jnp.dot(pc.astype(BF16), vc_ref[head, :].astype(BF16), preferred_element_type=F32)
               + jnp.dot(pn.astype(BF16), vn_ref[:, cols], preferred_element_type=F32))
        o12 = acc / l
        o_ref[:, cols] = _attn_finish(o12[:tq], o12[tq:], lam, lam_init, gsub_ref[...])


def _attn_cached(cst, lamp, gsub, qkv, ck, cv, layer, *, nb, t, nh):
    bt, d = qkv.shape[0], nh * LANES
    past = ck.shape[2]
    ck, cv = (c.reshape(c.shape[0], nb, past * nh, LANES) for c in (ck, cv))
    cache = pl.BlockSpec((None, None, past * nh, LANES), lambda b: (layer, b, 0, 0))

    def rows(group):
        return pl.BlockSpec((t, d), lambda b: (b, group))

    return pl.pallas_call(
        functools.partial(_attn_cached_kernel, tq=t, past=past, hd=LANES // 2),
        grid=(nb,),
        in_specs=[pl.BlockSpec(memory_space=pltpu.SMEM),
                  pl.BlockSpec(lamp.shape, lambda b: (0, 0)),
                  pl.BlockSpec((1, LANES), lambda b: (0, 0)),
                  rows(0), cache, cache, rows(1), rows(2)],
        out_specs=pl.BlockSpec((t, d), lambda b: (b, 0)),
        out_shape=jax.ShapeDtypeStruct((bt, d), BF16),
        compiler_params=_params("parallel"),
        name="attn_cached",
    )(cst, lamp, gsub, qkv, ck, cv, qkv, qkv)


def _split3(x):
    hi = x.astype(BF16)
    r = x - hi.astype(F32)
    mid = r.astype(BF16)
    lo = (r - mid.astype(F32)).astype(BF16)
    return hi, mid, lo


def _hgrn_kernel(*refs, heads, tt, blk, has_s0):
    n_in = 10 if has_s0 else 9
    for h in range(heads):
        cols = pl.ds(h * LANES, LANES)
        per_head = [r.at[:, cols] for r in refs[:5]] + list(refs[5:9])
        if has_s0:
            per_head.append(refs[9].at[h])
        o_ref, sf_ref, st_scr, pad_scr = refs[n_in:]
        _hgrn_head(*per_head, o_ref.at[:, cols], sf_ref.at[h], st_scr.at[h], pad_scr.at[h],
                   tt=tt, blk=blk, has_s0=has_s0)


def _hgrn_head(*refs, tt, blk, has_s0):
    hq_ref, hf_ref, hi_ref, hg_ref, lb_ref, gn_ref, tri_ref, ones2_ref, lvl_ref = refs[:9]
    s0_ref = refs[9] if has_s0 else None
    o_ref, sf_ref, st_scr, pad_scr = refs[-4:]
    t = pl.program_id(2)

    @pl.when(t == 0)
    def _():
        if has_s0:
            st_scr[...] = s0_ref[...].T
        else:
            st_scr[...] = jnp.zeros_like(st_scr)
        pad_scr[:, 0:PAD, :] = jnp.zeros((3, PAD, LANES), F32)

    lb = lb_ref[...]
    sig = jax.nn.sigmoid(hf_ref[...])
    logf = jnp.log(jnp.maximum(lb + (1.0 - lb) * sig, TINY)) * math.log2(math.e)
    kk = (1.0 - lb) * (1.0 - sig)
    qq = _silu(hq_ref[...])
    vv = hi_ref[...]

    tri = tri_ref[...]
    parts = _split3(logf)
    b = jnp.concatenate(
        [sum(jnp.dot(tri, part[c * blk:(c + 1) * blk], preferred_element_type=F32)
             for part in parts) for c in range(tt // blk)], axis=0)

    pad_scr[0, PAD:, :] = kk
    pad_scr[1, PAD:, :] = b
    pad_scr[2, PAD:, :] = vv
    ones2 = ones2_ref[...]
    pos = lax.broadcasted_iota(jnp.int32, (PAD, LANES), 0) % SUB

    def lag_terms(d):
        if d == 0:
            return (qq * kk).astype(BF16), vv
        ks, bs, vs = (pad_scr[r, PAD - d:PAD - d + tt, :] for r in range(3))
        off = jnp.tile(jnp.where(pos >= d, 0.0, MASK_VALUE), (tt // PAD, 1))
        return (qq * ks * jnp.exp2(b - bs + off)).astype(BF16), vs

    o = jnp.zeros((tt, LANES), F32)
    for d in range(0, SUB, 2):
        (p0, v0), (p1, v1) = lag_terms(d), lag_terms(d + 1)
        r = jnp.dot(jnp.concatenate([p0, p1], axis=1), ones2, preferred_element_type=F32)
        o = o + r[:, :LANES] * v0 + r[:, LANES:] * v1

    st = st_scr[...]
    older = []
    for c in range(tt // blk):
        sl = slice(c * blk, (c + 1) * blk)
        bc = b[sl]
        b_last = bc[blk - 1:blk]
        qt = (qq[sl] * jnp.exp2(bc)).astype(BF16)
        older.append(lax.dot_general(qt, st.astype(BF16), _NT, preferred_element_type=F32))
        kt = (kk[sl] * jnp.exp2(b_last - bc)).astype(BF16)
        st = st * jnp.exp2(b_last) + lax.dot_general(vv[sl].astype(BF16), kt, _TN,
                                                    preferred_element_type=F32)
    st_scr[...] = st
    o = o + jnp.concatenate(older, axis=0)

    weights = [None] * (tt // blk)
    m, level = SUB, 0
    while 2 * m <= blk:
        b3 = b.reshape(tt // (2 * m), 2 * m, LANES)
        w = jnp.exp2(-jnp.abs(b3 - b3[:, m - 1:m, :])).reshape(tt, LANES)
        qt = (qq * w).astype(BF16)
        kt = (kk * w).astype(BF16)
        for c in range(tt // blk):
            sl = slice(c * blk, (c + 1) * blk)
            a = lax.dot_general(qt[sl], kt[sl], _NT, preferred_element_type=F32) * lvl_ref[level]
            weights[c] = a if weights[c] is None else weights[c] + a
        m, level = 2 * m, level + 1
    if level:
        o = o + jnp.concatenate(
            [jnp.dot(weights[c].astype(BF16), vv[c * blk:(c + 1) * blk].astype(BF16),
                     preferred_element_type=F32) for c in range(tt // blk)], axis=0)

    y = o * lax.rsqrt(jnp.mean(o * o, axis=-1, keepdims=True) + EPS) * gn_ref[...]
    o_ref[...] = (y * _silu(hg_ref[...])).astype(BF16)

    @pl.when(t == pl.num_programs(2) - 1)
    def _():
        sf_ref[...] = st.T


def _hgrn(z, lb, gn, s0, layer, *, nb, t, nh, tt, blk, heads):
    bt = z.shape[0]
    nt = t // tt
    has_s0 = s0 is not None
    ng = nh // heads
    wide = heads * LANES

    def col(group):
        return pl.BlockSpec((tt, wide), lambda b, h, i: (b * nt + i, group * ng + h))

    state = pl.BlockSpec((None, heads, LANES, LANES), lambda b, h, i: (b, h, 0, 0))
    tri = jnp.tril(jnp.ones((blk, blk), BF16))
    ones2 = jnp.kron(jnp.eye(2, dtype=BF16), jnp.ones((LANES, LANES), BF16))
    pos = jnp.arange(blk)
    sizes = [SUB << i for i in range(max(blk // SUB, 1).bit_length() - 1)]
    lvl = jnp.stack([((pos[:, None] // (2 * m) == pos[None, :] // (2 * m))
                      & (pos[:, None] % (2 * m) >= m) & (pos[None, :] % (2 * m) < m))
                     for m in sizes]).astype(F32) if sizes else jnp.zeros((1, blk, blk), F32)
    in_specs = [col(0), col(1), col(2), col(3),
                pl.BlockSpec((1, wide), lambda b, h, i: (0, h)),
                pl.BlockSpec((1, LANES), lambda b, h, i: (0, 0)),
                _resident(tri.shape), _resident(ones2.shape), _resident(lvl.shape)]
    args = [z, z, z, z, lb, gn, tri, ones2, lvl]
    if has_s0:
        in_specs.append(pl.BlockSpec((None, None, heads, LANES, LANES),
                                     lambda b, h, i: (layer, b, h, 0, 0)))
        args.append(s0)
    return pl.pallas_call(
        functools.partial(_hgrn_kernel, heads=heads, tt=tt, blk=blk, has_s0=has_s0),
        grid=(nb, ng, nt),
        in_specs=in_specs,
        out_specs=[pl.BlockSpec((tt, wide), lambda b, h, i: (b * nt + i, h)), state],
        out_shape=[jax.ShapeDtypeStruct((bt, nh * LANES), BF16),
                   jax.ShapeDtypeStruct((nb, nh, LANES, LANES), F32)],
        scratch_shapes=[pltpu.VMEM((heads, LANES, LANES), F32),
                        pltpu.VMEM((heads, 3, PAD + tt, LANES), F32)],
        compiler_params=_params("parallel", "parallel", "arbitrary"),
        name="hgrn",
    )(*args)


def _merge_kernel(x_ref, gt_ref, oa_ref, oh_ref, za_ref, zh_ref, wa_ref, wh_ref, wo_ref, o_ref):
    a = jnp.dot(oa_ref[...], wa_ref[...], preferred_element_type=F32)
    r = jnp.dot(oh_ref[...], wh_ref[...], preferred_element_type=F32)
    merged = jax.nn.sigmoid(za_ref[...]) * a + jax.nn.sigmoid(zh_ref[...]) * r
    o_ref[...] = x_ref[...] + gt_ref[...] * jnp.dot(merged.astype(BF16), wo_ref[...],
                                                    preferred_element_type=F32)


def _merge(x, gt, oa, oh, z, wa, wh, wo, *, tm, per_row, rows_per_seq):
    t, d = x.shape
    mod = _mod_spec(per_row, tm, d, rows_per_seq)
    row = pl.BlockSpec((tm, d), lambda i: (i, 0))
    return pl.pallas_call(
        _merge_kernel,
        grid=(t // tm,),
        in_specs=[row, mod, row, row,
                  pl.BlockSpec((tm, d), lambda i: (i, 4)), pl.BlockSpec((tm, d), lambda i: (i, 5)),
                  _resident(wa.shape), _resident(wh.shape), _resident(wo.shape)],
        out_specs=row,
        out_shape=jax.ShapeDtypeStruct((t, d), F32),
        compiler_params=_params("parallel"),
        name="merge",
    )(x, gt, oa, oh, z, z, wa, wh, wo)


def _tile(n, target):
    if n <= target:
        return n
    for c in range(target, 7, -1):
        if n % c == 0 and c % 8 == 0:
            return c
    return n


def _layer_weights(p, l, d, fc):
    f = p["w_ffn1_d"].shape[1]
    nc = f // fc

    def ffn(w_gu, w_d):
        wg = w_gu[l, :, :f].reshape(d, nc, fc).transpose(1, 0, 2).astype(BF16)
        wu = w_gu[l, :, f:].reshape(d, nc, fc).transpose(1, 0, 2).astype(BF16)
        return wg, wu, w_d[l].reshape(nc, fc, d).astype(BF16)

    ng = p["w_in"].shape[2] // d
    return dict(
        ffn1=ffn(p["w_ffn1_gu"], p["w_ffn1_d"]),
        ffn2=ffn(p["w_ffn2_gu"], p["w_ffn2_d"]),
        w_in=p["w_in"][l].reshape(d, ng, d).transpose(1, 0, 2).astype(BF16),
        w_ba=p["w_br_att"][l].astype(BF16),
        w_bh=p["w_br_hg"][l].astype(BF16),
        w_out=p["w_out"][l].astype(BF16),
    )


def _trunk(x3, mods, p, lb_all, weights, cache):
    nb, t, d = x3.shape
    depth = p["w_ada"].shape[0]
    nh = d // LANES
    bt = nb * t
    x = x3.reshape(bt, d)
    per_row = t < 256
    tm = _tile(bt if per_row else t, 512)
    kst = jnp.zeros((depth, bt * nh, LANES), F32)
    vst = jnp.zeros((depth, bt * nh, LANES), F32)
    states = []
    for l in range(depth):
        w = weights[l]
        m = mods[l]
        if per_row:
            mm = [jnp.repeat(m[:, i], t, axis=0) for i in range(N_ADA)]
        else:
            mm = [m[:, i].reshape(nb, 1, d) for i in range(N_ADA)]
        sh1, sc1, g1, sh2, sc2, g2, sh3, sc3, g3 = mm
        kw = dict(tm=tm, per_row=per_row, rows_per_seq=t)
        row = lambda a: a.reshape(1, -1)

        g_final = row(p["g_final"])
        x = _ffn(x, row(p["g_ffn1"][l]), sh1, sc1, g1, *w["ffn1"], g_final, final_norm=False, **kw)
        q_scale = (LANES // 2) ** -0.5 * math.log2(math.e)
        kw_in = dict(kw, tm=_tile(tm, 256))
        z, qkv, kst, vst = _inproj(x, row(p["g_mix"][l]), sh2, sc2, w["w_in"], kst, vst, l,
                                   q_scale=q_scale, **kw_in)

        lam_init = 0.8 - 0.6 * math.exp(-0.3 * l)
        cst = jnp.array([lam_init], F32)
        gsub = row(p["g_att_sub"][l])
        if cache is None:
            oa = _attn(cst, p["att_lambda"][l], gsub, qkv, l, nb=nb, t=t, nh=nh,
                       tq=_tile(t, Q_TILE))
            s0 = None
        else:
            ck, cv, s0 = cache
            oa = _attn_cached(cst, p["att_lambda"][l], gsub, qkv, ck, cv, l, nb=nb, t=t, nh=nh)
        tt = _tile(t, 2048)
        blk = min(tt, 128)
        oh, s_fin = _hgrn(z, row(lb_all[l]), row(p["g_hg_norm"][l]), s0, l,
                          nb=nb, t=t, nh=nh, tt=tt, blk=blk, heads=nh if per_row else 1)
        states.append(s_fin)

        x = _merge(x, g2, oa, oh, z, w["w_ba"], w["w_bh"], w["w_out"], **kw)
        x = _ffn(x, row(p["g_ffn2"][l]), sh3, sc3, g3, *w["ffn2"], g_final,
                 final_norm=l == depth - 1, **kw)

    y = x.reshape(nb, t, d)
    shape5 = (depth, nb, t, nh, LANES)
    return y, kst.reshape(shape5), vst.reshape(shape5), jnp.stack(states)


def kernel(x_prompt, x_sample, cache_k, cache_v, state_hgrn, c_prompt, c_sample, w_ada, b_ada, g_ffn1, w_ffn1_gu, w_ffn1_d, g_mix, w_in, att_lambda, g_att_sub, hg_lb_logits, g_hg_norm, w_br_att, w_br_hg, w_out, g_ffn2, w_ffn2_gu, w_ffn2_d, g_final):
    p = dict(w_ada=w_ada, b_ada=b_ada, g_ffn1=g_ffn1, w_ffn1_gu=w_ffn1_gu, w_ffn1_d=w_ffn1_d,
             g_mix=g_mix, w_in=w_in, att_lambda=att_lambda, g_att_sub=g_att_sub,
             g_hg_norm=g_hg_norm, w_br_att=w_br_att, w_br_hg=w_br_hg, w_out=w_out,
             g_ffn2=g_ffn2, w_ffn2_gu=w_ffn2_gu, w_ffn2_d=w_ffn2_d, g_final=g_final)
    depth, d, _ = w_ada.shape
    nbp, nbs = x_prompt.shape[0], x_sample.shape[0]

    lb_sm = jax.nn.softmax(hg_lb_logits.astype(F32), axis=0)
    lb_all = jnp.cumsum(lb_sm, axis=0) - lb_sm[0]

    ns = nbp + nbs
    nsp = -(-ns // 8) * 8
    c_all = jnp.concatenate([c_prompt, c_sample, jnp.zeros((nsp - ns, d), F32)], axis=0)
    mods = _ada(c_all, w_ada, b_ada).reshape(depth, nsp, N_ADA, d)

    f = w_ffn1_d.shape[1]
    fc = 256 if f % 256 == 0 else LANES
    weights = [_layer_weights(p, l, d, fc) for l in range(depth)]

    y_p, k_p, v_p, s_p = _trunk(x_prompt, mods[:, :nbp], p, lb_all, weights, None)
    y_s, k_s, v_s, s_s = _trunk(x_sample, mods[:, nbp:ns], p, lb_all, weights,
                                (cache_k, cache_v, state_hgrn))
    return (y_p, y_s, k_p, v_p, s_p, k_s, v_s, s_s)
```

```python
import functools
import math

import jax
import jax.numpy as jnp
from jax import lax
from jax.experimental import pallas as pl
from jax.experimental.pallas import tpu as pltpu

F32 = jnp.float32
BF16 = jnp.bfloat16

CHUNK = 64
N_ADA = 9
MACARON_WEIGHT = 0.5
EPS = 1e-6
MASK_VALUE = -1e30
TINY = 1e-30
LANES = 128
SUB = 4
PAD = 8
VMEM_LIMIT = 56 * 1024 * 1024
FFN_TILE = 1024
HGRN_TILE = 4096
Q_TILE = 512
KV_TILE = 2048
KV_STEPS_PER_TRIP = 2

_NT = (((1,), (1,)), ((), ()))
_TN = (((0,), (0,)), ((), ()))


def _params(*sem):
    return pltpu.CompilerParams(dimension_semantics=sem, vmem_limit_bytes=VMEM_LIMIT)


def _resident(shape):
    nd = len(shape)
    return pl.BlockSpec(shape, lambda *_: (0,) * nd, pipeline_mode=pl.Buffered(1))


def _modnorm(x, g, sc, sh):
    y = x * lax.rsqrt(jnp.mean(x * x, axis=-1, keepdims=True) + EPS) * g
    return y * (1.0 + sc) + sh


def _silu(x):
    return x * jax.nn.sigmoid(x)


def _mod_spec(per_row, tm, d, rows_per_seq):
    if per_row:
        return pl.BlockSpec((tm, d), lambda i, *_: (i, 0))
    tiles = rows_per_seq // tm
    return pl.BlockSpec((None, 1, d), lambda i, *_: (i // tiles, 0, 0))


def _ada_kernel(c_ref, w_ref, b_ref, o_ref):
    c = c_ref[...]
    o_ref[...] = jnp.dot(_silu(c).astype(BF16), w_ref[...].astype(BF16),
                         preferred_element_type=F32) + b_ref[...]


def _ada(c_all, w_ada, b_ada):
    depth, d, _ = w_ada.shape
    ns = c_all.shape[0]
    wide = 3 * d
    return pl.pallas_call(
        _ada_kernel,
        grid=(depth, N_ADA * d // wide),
        in_specs=[pl.BlockSpec((ns, d), lambda l, j: (0, 0)),
                  pl.BlockSpec((None, d, wide), lambda l, j: (l, 0, j)),
                  pl.BlockSpec((None, 1, wide), lambda l, j: (l, 0, j))],
        out_specs=pl.BlockSpec((None, ns, wide), lambda l, j: (l, 0, j)),
        out_shape=jax.ShapeDtypeStruct((depth, ns, N_ADA * d), F32),
        compiler_params=_params("parallel", "parallel"),
        name="ada",
    )(c_all, w_ada, b_ada.reshape(depth, 1, N_ADA * d))


def _ffn_kernel(x_ref, g_ref, sh_ref, sc_ref, gt_ref, wg_ref, wu_ref, wd_ref, gf_ref, o_ref,
                h_scr, acc_scr, *, n_chunks, final_norm):
    h_scr[...] = _modnorm(x_ref[...], g_ref[...], sc_ref[...], sh_ref[...]).astype(BF16)
    for c in range(n_chunks):
        h = h_scr[...]
        gate = jnp.dot(h, wg_ref[c], preferred_element_type=F32)
        up = jnp.dot(h, wu_ref[c], preferred_element_type=F32)
        a = (_silu(gate) * up).astype(BF16)
        down = jnp.dot(a, wd_ref[c], preferred_element_type=F32)
        if c == 0:
            acc_scr[...] = down
        else:
            acc_scr[...] += down
    y = x_ref[...] + MACARON_WEIGHT * gt_ref[...] * acc_scr[...]
    if final_norm:
        y = y * lax.rsqrt(jnp.mean(y * y, axis=-1, keepdims=True) + EPS) * gf_ref[...]
    o_ref[...] = y


def _ffn(x, g, sh, sc, gt, wg, wu, wd, g_final, *, tm, per_row, rows_per_seq, final_norm):
    t, d = x.shape
    n_chunks = wg.shape[0]
    mod = _mod_spec(per_row, tm, d, rows_per_seq)
    row = pl.BlockSpec((tm, d), lambda i: (i, 0))
    gain = pl.BlockSpec((1, d), lambda i: (0, 0))
    return pl.pallas_call(
        functools.partial(_ffn_kernel, n_chunks=n_chunks, final_norm=final_norm),
        grid=(t // tm,),
        in_specs=[row, gain, mod, mod, mod,
                  _resident(wg.shape), _resident(wu.shape), _resident(wd.shape), gain],
        out_specs=row,
        out_shape=jax.ShapeDtypeStruct((t, d), F32),
        scratch_shapes=[pltpu.VMEM((tm, d), BF16), pltpu.VMEM((tm, d), F32)],
        compiler_params=_params("parallel"),
        name="ffn",
    )(x, g, sh, sc, gt, wg, wu, wd, g_final)


N_QKV = 3


def _inproj_kernel(x_ref, g_ref, sh_ref, sc_ref, w_ref, kin_ref, vin_ref,
                   z_ref, qkv_ref, k_ref, v_ref, u_scr, *, q_scale):
    del kin_ref, vin_ref
    d = u_scr.shape[1]

    def group(g):
        return jnp.dot(u_scr[...], w_ref[g], preferred_element_type=F32)

    def cache_rows(ref, z):
        tm, nh = z.shape[0], z.shape[1] // LANES
        for h in range(nh):
            ref[pl.ds(h, tm, stride=nh), :] = z[:, h * LANES:(h + 1) * LANES]

    u_scr[...] = _modnorm(x_ref[...], g_ref[...], sc_ref[...], sh_ref[...]).astype(BF16)
    qkv_ref[:, 0:d] = (group(0) * q_scale).astype(BF16)
    for g, ref in ((1, k_ref), (2, v_ref)):
        z = group(g)
        cache_rows(ref, z)
        qkv_ref[:, g * d:(g + 1) * d] = z.astype(BF16)
    for g in range(N_QKV, w_ref.shape[0]):
        z_ref[:, (g - N_QKV) * d:(g - N_QKV + 1) * d] = group(g)


def _inproj(x, g, sh, sc, w3, kst, vst, layer, *, tm, per_row, rows_per_seq, q_scale):
    t, d = x.shape
    ng = w3.shape[0]
    mod = _mod_spec(per_row, tm, d, rows_per_seq)
    row = pl.BlockSpec((tm, d), lambda i: (i, 0))
    stack = pl.BlockSpec((None, tm * (d // LANES), LANES), lambda i: (layer, i, 0))
    anyspec = pl.BlockSpec(memory_space=pl.ANY)
    return pl.pallas_call(
        functools.partial(_inproj_kernel, q_scale=q_scale),
        grid=(t // tm,),
        in_specs=[row, pl.BlockSpec((1, d), lambda i: (0, 0)), mod, mod,
                  _resident(w3.shape), anyspec, anyspec],
        out_specs=[pl.BlockSpec((tm, (ng - N_QKV) * d), lambda i: (i, 0)),
                   pl.BlockSpec((tm, N_QKV * d), lambda i: (i, 0)),
                   stack, stack],
        out_shape=[jax.ShapeDtypeStruct((t, (ng - N_QKV) * d), F32),
                   jax.ShapeDtypeStruct((t, N_QKV * d), BF16),
                   jax.ShapeDtypeStruct(kst.shape, F32),
                   jax.ShapeDtypeStruct(vst.shape, F32)],
        scratch_shapes=[pltpu.VMEM((tm, d), BF16)],
        input_output_aliases={5: 2, 6: 3},
        compiler_params=_params("parallel"),
        name="inproj",
    )(x, g, sh, sc, w3, kst, vst)


def _lambda(lamp_ref, lam_init):
    lp = lamp_ref[...]
    s01 = jnp.sum(lp[0:1] * lp[1:2], axis=-1, keepdims=True)
    s23 = jnp.sum(lp[2:3] * lp[3:4], axis=-1, keepdims=True)
    return jnp.exp(s01) - jnp.exp(s23) + lam_init


def _split_maps(q, hd):
    lane = lax.broadcasted_iota(jnp.int32, q.shape, 1)
    zero = jnp.zeros_like(q)
    return jnp.concatenate([jnp.where(lane < hd, q, zero), jnp.where(lane >= hd, q, zero)],
                           axis=0).astype(BF16)


def _visible(rows, cols, tq, q0, k0):
    r = lax.broadcasted_iota(jnp.int32, (rows, cols), 0)
    qpos = q0 + jnp.where(r >= tq, r - tq, r)
    kpos = k0 + lax.broadcasted_iota(jnp.int32, (rows, cols), 1)
    return kpos < (qpos // CHUNK + 1) * CHUNK


def _attn_finish(o1, o2, lam, lam_init, gsub):
    o = o1 - lam * o2
    y = o * lax.rsqrt(jnp.mean(o * o, axis=-1, keepdims=True) + EPS) * gsub
    return (y * (1.0 - lam_init)).astype(BF16)


def _attn_kernel(cst_ref, lamp_ref, gsub_ref, bias_ref, q_ref, k_ref, v_ref, o_ref,
                 qs_scr, vx_scr, m_scr, acc_scr, *, tq, kw, hd):
    i = pl.program_id(2)
    lam_init = cst_ref[0]

    @pl.when(i == 0)
    def _():
        vx_scr[:, :LANES] = v_ref[...]
        vx_scr[:, LANES:] = jnp.ones((vx_scr.shape[0], LANES), BF16)

    q = q_ref[...]
    lane = lax.broadcasted_iota(jnp.int32, q.shape, 1)
    qs_scr[:tq] = jnp.where(lane < hd, q, jnp.zeros_like(q))
    qs_scr[tq:] = jnp.where(lane >= hd, q, jnp.zeros_like(q))
    def kv_step(start, width, masked, first=False):
        k = k_ref[pl.ds(start, width), :]
        vx = vx_scr[pl.ds(start, width), :]
        maps = (slice(0, tq), slice(tq, 2 * tq))
        scores = [lax.dot_general(qs_scr[rows], k, _NT, preferred_element_type=F32)
                  for rows in maps]
        for rows, s in zip(maps, scores):
            if masked:
                diag = s[:, width - tq:] + bias_ref[...]
                s = diag if width == tq else jnp.concatenate([s[:, :width - tq], diag], axis=1)
            m_cur = jnp.max(s, axis=-1, keepdims=True)
            m_new = jnp.broadcast_to(m_cur, (tq, LANES)) if first else jnp.maximum(m_scr[rows], m_cur)
            p = jnp.exp2(s - jnp.tile(m_new, (1, width // LANES))).astype(BF16)
            pv = jnp.dot(p, vx, preferred_element_type=F32)
            if first:
                acc_scr[rows] = pv
            else:
                alpha = jnp.exp2(m_scr[rows] - m_new)
                acc_scr[rows] = jnp.tile(alpha, (1, 2)) * acc_scr[rows] + pv
            m_scr[rows] = m_new

    def full_step(j):
        kv_step(pl.multiple_of(j * kw, kw), kw, False)

    n_full = (i * tq) // kw

    for pieces in range(kw // tq):
        for left in range(KV_STEPS_PER_TRIP):
            @pl.when((i % (kw // tq) == pieces) & (n_full % KV_STEPS_PER_TRIP == left))
            def _():
                kv_step(pl.multiple_of((i - pieces) * tq, tq), (pieces + 1) * tq, True, first=True)
                for s in range(left):
                    full_step(n_full - left + s)

    def full_body(k, carry):
        for s in range(KV_STEPS_PER_TRIP):
            full_step(KV_STEPS_PER_TRIP * k + s)
        return carry

    lax.fori_loop(0, n_full // KV_STEPS_PER_TRIP, full_body, 0)

    acc = acc_scr[...]
    o12 = acc[:, :LANES] / acc[:, LANES:]
    lam = _lambda(lamp_ref, lam_init)
    o_ref[...] = _attn_finish(o12[:tq], o12[tq:], lam, lam_init, gsub_ref[...])


def _attn(cst, lamp, gsub, qkv, layer, *, nb, t, nh, tq):
    del layer
    bt, d = qkv.shape[0], nh * LANES
    nq = t // tq
    kw = KV_TILE if KV_TILE % tq == 0 and KV_TILE <= t else tq
    assert tq % CHUNK == 0 or nq == 1, (tq, CHUNK)
    chunk = jnp.arange(tq) // CHUNK
    bias = jnp.where(chunk[None, :] <= chunk[:, None], 0.0, MASK_VALUE).astype(F32)

    def kv(group):
        return pl.BlockSpec((t, LANES), lambda b, h, i: (b, group * nh + h))

    return pl.pallas_call(
        functools.partial(_attn_kernel, tq=tq, kw=kw, hd=LANES // 2),
        grid=(nb, nh, nq),
        in_specs=[pl.BlockSpec(memory_space=pltpu.SMEM),
                  pl.BlockSpec(lamp.shape, lambda b, h, i: (0, 0)),
                  pl.BlockSpec((1, LANES), lambda b, h, i: (0, 0)),
                  _resident(bias.shape),
                  pl.BlockSpec((tq, LANES), lambda b, h, i: (b * nq + i, h)),
                  kv(1), kv(2)],
        out_specs=pl.BlockSpec((tq, LANES), lambda b, h, i: (b * nq + i, h)),
        out_shape=jax.ShapeDtypeStruct((bt, d), BF16),
        scratch_shapes=[pltpu.VMEM((2 * tq, LANES), BF16), pltpu.VMEM((t, 2 * LANES), BF16),
                        pltpu.VMEM((2 * tq, LANES), F32), pltpu.VMEM((2 * tq, 2 * LANES), F32)],
        compiler_params=_params("parallel", "parallel", "arbitrary"),
        name="attn",
    )(cst, lamp, gsub, bias, qkv, qkv, qkv)


def _attn_cached_kernel(cst_ref, lamp_ref, gsub_ref, q_ref, kc_ref, vc_ref, kn_ref, vn_ref, o_ref,
                        *, tq, past, hd):
    lam_init = cst_ref[0]
    lam = _lambda(lamp_ref, lam_init)
    nh = q_ref.shape[1] // LANES
    masked = past % CHUNK + tq > CHUNK
    for h in range(nh):
        cols = slice(h * LANES, (h + 1) * LANES)
        head = pl.ds(h, past, stride=nh)
        qs = _split_maps(q_ref[:, cols], hd)

        def scores(k, k0):
            s = lax.dot_general(qs, k.astype(BF16), _NT, preferred_element_type=F32)
            if masked:
                s = jnp.where(_visible(2 * tq, k.shape[0], tq, past, k0), s, MASK_VALUE)
            return s

        sc = scores(kc_ref[head, :], 0)
        sn = scores(kn_ref[:, cols], past)
        m = jnp.maximum(jnp.max(sc, axis=-1, keepdims=True), jnp.max(sn, axis=-1, keepdims=True))
        pc = jnp.exp2(sc - m)
        pn = jnp.exp2(sn - m)
        l = jnp.sum(pc, axis=-1, keepdims=True) + jnp.sum(pn, axis=-1, keepdims=True)
        acc = (jnp.dot(pc.astype(BF16), vc_ref[head, :].astype(BF16), preferred_element_type=F32)
               + jnp.dot(pn.astype(BF16), vn_ref[:, cols], preferred_element_type=F32))
        o12 = acc / l
        o_ref[:, cols] = _attn_finish(o12[:tq], o12[tq:], lam, lam_init, gsub_ref[...])


def _attn_cached(cst, lamp, gsub, qkv, ck, cv, layer, *, nb, t, nh):
    bt, d = qkv.shape[0], nh * LANES
    past = ck.shape[2]
    ck, cv = (c.reshape(c.shape[0], nb, past * nh, LANES) for c in (ck, cv))
    cache = pl.BlockSpec((None, None, past * nh, LANES), lambda b: (layer, b, 0, 0))

    def rows(group):
        return pl.BlockSpec((t, d), lambda b: (b, group))

    return pl.pallas_call(
        functools.partial(_attn_cached_kernel, tq=t, past=past, hd=LANES // 2),
        grid=(nb,),
        in_specs=[pl.BlockSpec(memory_space=pltpu.SMEM),
                  pl.BlockSpec(lamp.shape, lambda b: (0, 0)),
                  pl.BlockSpec((1, LANES), lambda b: (0, 0)),
                  rows(0), cache, cache, rows(1), rows(2)],
        out_specs=pl.BlockSpec((t, d), lambda b: (b, 0)),
        out_shape=jax.ShapeDtypeStruct((bt, d), BF16),
        compiler_params=_params("parallel"),
        name="attn_cached",
    )(cst, lamp, gsub, qkv, ck, cv, qkv, qkv)


def _split3(x):
    hi = x.astype(BF16)
    r = x - hi.astype(F32)
    mid = r.astype(BF16)
    lo = (r - mid.astype(F32)).astype(BF16)
    return hi, mid, lo


def _hgrn_kernel(*refs, heads, tt, blk, has_s0):
    n_in = 10 if has_s0 else 9
    for h in range(heads):
        cols = pl.ds(h * LANES, LANES)
        per_head = [r.at[:, cols] for r in refs[:5]] + list(refs[5:9])
        if has_s0:
            per_head.append(refs[9].at[h])
        o_ref, sf_ref, st_scr, pad_scr = refs[n_in:]
        _hgrn_head(*per_head, o_ref.at[:, cols], sf_ref.at[h], st_scr.at[h], pad_scr.at[h],
                   tt=tt, blk=blk, has_s0=has_s0)


def _hgrn_head(*refs, tt, blk, has_s0):
    hq_ref, hf_ref, hi_ref, hg_ref, lb_ref, gn_ref, tri_ref, ones2_ref, lvl_ref = refs[:9]
    s0_ref = refs[9] if has_s0 else None
    o_ref, sf_ref, st_scr, pad_scr = refs[-4:]
    t = pl.program_id(2)

    @pl.when(t == 0)
    def _():
        if has_s0:
            st_scr[...] = s0_ref[...].T
        else:
            st_scr[...] = jnp.zeros_like(st_scr)
        pad_scr[:, 0:PAD, :] = jnp.zeros((3, PAD, LANES), F32)

    lb = lb_ref[...]
    sig = jax.nn.sigmoid(hf_ref[...])
    logf = jnp.log(jnp.maximum(lb + (1.0 - lb) * sig, TINY)) * math.log2(math.e)
    kk = (1.0 - lb) * (1.0 - sig)
    qq = _silu(hq_ref[...])
    vv = hi_ref[...]

    tri = tri_ref[...]
    parts = _split3(logf)
    b = jnp.concatenate(
        [sum(jnp.dot(tri, part[c * blk:(c + 1) * blk], preferred_element_type=F32)
             for part in parts) for c in range(tt // blk)], axis=0)

    pad_scr[0, PAD:, :] = kk
    pad_scr[1, PAD:, :] = b
    pad_scr[2, PAD:, :] = vv
    ones2 = ones2_ref[...]
    pos = lax.broadcasted_iota(jnp.int32, (PAD, LANES), 0) % SUB

    def lag_terms(d):
        if d == 0:
            return (qq * kk).astype(BF16), vv
        ks, bs, vs = (pad_scr[r, PAD - d:PAD - d + tt, :] for r in range(3))
        off = jnp.tile(jnp.where(pos >= d, 0.0, MASK_VALUE), (tt // PAD, 1))
        return (qq * ks * jnp.exp2(b - bs + off)).astype(BF16), vs

    o = jnp.zeros((tt, LANES), F32)
    for d in range(0, SUB, 2):
        (p0, v0), (p1, v1) = lag_terms(d), lag_terms(d + 1)
        r = jnp.dot(jnp.concatenate([p0, p1], axis=1), ones2, preferred_element_type=F32)
        o = o + r[:, :LANES] * v0 + r[:, LANES:] * v1

    st = st_scr[...]
    older = []
    for c in range(tt // blk):
        sl = slice(c * blk, (c + 1) * blk)
        bc = b[sl]
        b_last = bc[blk - 1:blk]
        qt = (qq[sl] * jnp.exp2(bc)).astype(BF16)
        older.append(lax.dot_general(qt, st.astype(BF16), _NT, preferred_element_type=F32))
        kt = (kk[sl] * jnp.exp2(b_last - bc)).astype(BF16)
        st = st * jnp.exp2(b_last) + lax.dot_general(vv[sl].astype(BF16), kt, _TN,
                                                    preferred_element_type=F32)
    st_scr[...] = st
    o = o + jnp.concatenate(older, axis=0)

    weights = [None] * (tt // blk)
    m, level = SUB, 0
    while 2 * m <= blk:
        b3 = b.reshape(tt // (2 * m), 2 * m, LANES)
        w = jnp.exp2(-jnp.abs(b3 - b3[:, m - 1:m, :])).reshape(tt, LANES)
        qt = (qq * w).astype(BF16)
        kt = (kk * w).astype(BF16)
        for c in range(tt // blk):
            sl = slice(c * blk, (c + 1) * blk)
            a = lax.dot_general(qt[sl], kt[sl], _NT, preferred_element_type=F32) * lvl_ref[level]
            weights[c] = a if weights[c] is None else weights[c] + a
        m, level = 2 * m, level + 1
    if level:
        o = o + jnp.concatenate(
            [jnp.dot(weights[c].astype(BF16), vv[c * blk:(c + 1) * blk].astype(BF16),
                     preferred_element_type=F32) for c in range(tt // blk)], axis=0)

    y = o * lax.rsqrt(jnp.mean(o * o, axis=-1, keepdims=True) + EPS) * gn_ref[...]
    o_ref[...] = (y * _silu(hg_ref[...])).astype(BF16)

    @pl.when(t == pl.num_programs(2) - 1)
    def _():
        sf_ref[...] = st.T


def _hgrn(z, lb, gn, s0, layer, *, nb, t, nh, tt, blk, heads):
    bt = z.shape[0]
    nt = t // tt
    has_s0 = s0 is not None
    ng = nh // heads
    wide = heads * LANES

    def col(group):
        return pl.BlockSpec((tt, wide), lambda b, h, i: (b * nt + i, group * ng + h))

    state = pl.BlockSpec((None, heads, LANES, LANES), lambda b, h, i: (b, h, 0, 0))
    tri = jnp.tril(jnp.ones((blk, blk), BF16))
    ones2 = jnp.kron(jnp.eye(2, dtype=BF16), jnp.ones((LANES, LANES), BF16))
    pos = jnp.arange(blk)
    sizes = [SUB << i for i in range(max(blk // SUB, 1).bit_length() - 1)]
    lvl = jnp.stack([((pos[:, None] // (2 * m) == pos[None, :] // (2 * m))
                      & (pos[:, None] % (2 * m) >= m) & (pos[None, :] % (2 * m) < m))
                     for m in sizes]).astype(F32) if sizes else jnp.zeros((1, blk, blk), F32)
    in_specs = [col(0), col(1), col(2), col(3),
                pl.BlockSpec((1, wide), lambda b, h, i: (0, h)),
                pl.BlockSpec((1, LANES), lambda b, h, i: (0, 0)),
                _resident(tri.shape), _resident(ones2.shape), _resident(lvl.shape)]
    args = [z, z, z, z, lb, gn, tri, ones2, lvl]
    if has_s0:
        in_specs.append(pl.BlockSpec((None, None, heads, LANES, LANES),
                                     lambda b, h, i: (layer, b, h, 0, 0)))
        args.append(s0)
    return pl.pallas_call(
        functools.partial(_hgrn_kernel, heads=heads, tt=tt, blk=blk, has_s0=has_s0),
        grid=(nb, ng, nt),
        in_specs=in_specs,
        out_specs=[pl.BlockSpec((tt, wide), lambda b, h, i: (b * nt + i, h)), state],
        out_shape=[jax.ShapeDtypeStruct((bt, nh * LANES), BF16),
                   jax.ShapeDtypeStruct((nb, nh, LANES, LANES), F32)],
        scratch_shapes=[pltpu.VMEM((heads, LANES, LANES), F32),
                        pltpu.VMEM((heads, 3, PAD + tt, LANES), F32)],
        compiler_params=_params("parallel", "parallel", "arbitrary"),
        name="hgrn",
    )(*args)


def _merge_kernel(x_ref, gt_ref, oa_ref, oh_ref, za_ref, zh_ref, wa_ref, wh_ref, wo_ref, o_ref):
    a = jnp.dot(oa_ref[...], wa_ref[...], preferred_element_type=F32)
    r = jnp.dot(oh_ref[...], wh_ref[...], preferred_element_type=F32)
    merged = jax.nn.sigmoid(za_ref[...]) * a + jax.nn.sigmoid(zh_ref[...]) * r
    o_ref[...] = x_ref[...] + gt_ref[...] * jnp.dot(merged.astype(BF16), wo_ref[...],
                                                    preferred_element_type=F32)


def _merge(x, gt, oa, oh, z, wa, wh, wo, *, tm, per_row, rows_per_seq):
    t, d = x.shape
    mod = _mod_spec(per_row, tm, d, rows_per_seq)
    row = pl.BlockSpec((tm, d), lambda i: (i, 0))
    return pl.pallas_call(
        _merge_kernel,
        grid=(t // tm,),
        in_specs=[row, mod, row, row,
                  pl.BlockSpec((tm, d), lambda i: (i, 4)), pl.BlockSpec((tm, d), lambda i: (i, 5)),
                  _resident(wa.shape), _resident(wh.shape), _resident(wo.shape)],
        out_specs=row,
        out_shape=jax.ShapeDtypeStruct((t, d), F32),
        compiler_params=_params("parallel"),
        name="merge",
    )(x, gt, oa, oh, z, z, wa, wh, wo)


def _tile(n, target):
    if n <= target:
        return n
    for c in range(target, 7, -1):
        if n % c == 0 and c % 8 == 0:
            return c
    return n


def _layer_weights(p, l, d, fc):
    f = p["w_ffn1_d"].shape[1]
    nc = f // fc

    def ffn(w_gu, w_d):
        wg = w_gu[l, :, :f].reshape(d, nc, fc).transpose(1, 0, 2).astype(BF16)
        wu = w_gu[l, :, f:].reshape(d, nc, fc).transpose(1, 0, 2).astype(BF16)
        return wg, wu, w_d[l].reshape(nc, fc, d).astype(BF16)

    ng = p["w_in"].shape[2] // d
    return dict(
        ffn1=ffn(p["w_ffn1_gu"], p["w_ffn1_d"]),
        ffn2=ffn(p["w_ffn2_gu"], p["w_ffn2_d"]),
        w_in=p["w_in"][l].reshape(d, ng, d).transpose(1, 0, 2).astype(BF16),
        w_ba=p["w_br_att"][l].astype(BF16),
        w_bh=p["w_br_hg"][l].astype(BF16),
        w_out=p["w_out"][l].astype(BF16),
    )


def _trunk(x3, mods, p, lb_all, weights, cache):
    nb, t, d = x3.shape
    depth = p["w_ada"].shape[0]
    nh = d // LANES
    bt = nb * t
    x = x3.reshape(bt, d)
    per_row = t < 256
    tm = _tile(bt if per_row else t, 512)
    kst = jnp.zeros((depth, bt * nh, LANES), F32)
    vst = jnp.zeros((depth, bt * nh, LANES), F32)
    states = []
    for l in range(depth):
        w = weights[l]
        m = mods[l]
        if per_row:
            mm = [jnp.repeat(m[:, i], t, axis=0) for i in range(N_ADA)]
        else:
            mm = [m[:, i].reshape(nb, 1, d) for i in range(N_ADA)]
        sh1, sc1, g1, sh2, sc2, g2, sh3, sc3, g3 = mm
        kw = dict(tm=tm, per_row=per_row, rows_per_seq=t)
        row = lambda a: a.reshape(1, -1)

        g_final = row(p["g_final"])
        kw_ffn = dict(kw, tm=_tile(bt if per_row else t, FFN_TILE))
        x = _ffn(x, row(p["g_ffn1"][l]), sh1, sc1, g1, *w["ffn1"], g_final, final_norm=False,
                 **kw_ffn)
        q_scale = (LANES // 2) ** -0.5 * math.log2(math.e)
        kw_in = dict(kw, tm=_tile(tm, 256))
        z, qkv, kst, vst = _inproj(x, row(p["g_mix"][l]), sh2, sc2, w["w_in"], kst, vst, l,
                                   q_scale=q_scale, **kw_in)

        lam_init = 0.8 - 0.6 * math.exp(-0.3 * l)
        cst = jnp.array([lam_init], F32)
        gsub = row(p["g_att_sub"][l])
        if cache is None:
            oa = _attn(cst, p["att_lambda"][l], gsub, qkv, l, nb=nb, t=t, nh=nh,
                       tq=_tile(t, Q_TILE))
            s0 = None
        else:
            ck, cv, s0 = cache
            oa = _attn_cached(cst, p["att_lambda"][l], gsub, qkv, ck, cv, l, nb=nb, t=t, nh=nh)
        tt = _tile(t, HGRN_TILE)
        blk = min(tt, 128)
        oh, s_fin = _hgrn(z, row(lb_all[l]), row(p["g_hg_norm"][l]), s0, l,
                          nb=nb, t=t, nh=nh, tt=tt, blk=blk, heads=nh if per_row else 1)
        states.append(s_fin)

        x = _merge(x, g2, oa, oh, z, w["w_ba"], w["w_bh"], w["w_out"], **kw)
        x = _ffn(x, row(p["g_ffn2"][l]), sh3, sc3, g3, *w["ffn2"], g_final,
                 final_norm=l == depth - 1, **kw_ffn)

    y = x.reshape(nb, t, d)
    shape5 = (depth, nb, t, nh, LANES)
    return y, kst.reshape(shape5), vst.reshape(shape5), jnp.stack(states)


def kernel(x_prompt, x_sample, cache_k, cache_v, state_hgrn, c_prompt, c_sample, w_ada, b_ada, g_ffn1, w_ffn1_gu, w_ffn1_d, g_mix, w_in, att_lambda, g_att_sub, hg_lb_logits, g_hg_norm, w_br_att, w_br_hg, w_out, g_ffn2, w_ffn2_gu, w_ffn2_d, g_final):
    p = dict(w_ada=w_ada, b_ada=b_ada, g_ffn1=g_ffn1, w_ffn1_gu=w_ffn1_gu, w_ffn1_d=w_ffn1_d,
             g_mix=g_mix, w_in=w_in, att_lambda=att_lambda, g_att_sub=g_att_sub,
             g_hg_norm=g_hg_norm, w_br_att=w_br_att, w_br_hg=w_br_hg, w_out=w_out,
             g_ffn2=g_ffn2, w_ffn2_gu=w_ffn2_gu, w_ffn2_d=w_ffn2_d, g_final=g_final)
    depth, d, _ = w_ada.shape
    nbp, nbs = x_prompt.shape[0], x_sample.shape[0]

    lb_sm = jax.nn.softmax(hg_lb_logits.astype(F32), axis=0)
    lb_all = jnp.cumsum(lb_sm, axis=0) - lb_sm[0]

    ns = nbp + nbs
    nsp = -(-ns // 8) * 8
    c_all = jnp.concatenate([c_prompt, c_sample, jnp.zeros((nsp - ns, d), F32)], axis=0)
    mods = _ada(c_all, w_ada, b_ada).reshape(depth, nsp, N_ADA, d)

    f = w_ffn1_d.shape[1]
    fc = 256 if f % 256 == 0 else LANES
    weights = [_layer_weights(p, l, d, fc) for l in range(depth)]

    y_p, k_p, v_p, s_p = _trunk(x_prompt, mods[:, :nbp], p, lb_all, weights, None)
    y_s, k_s, v_s, s_s = _trunk(x_sample, mods[:, nbp:ns], p, lb_all, weights,
                                (cache_k, cache_v, state_hgrn))
    return (y_p, y_s, k_p, v_p, s_p, k_s, v_s, s_s)
```

```python
import functools
import math

import jax
import jax.numpy as jnp
from jax import lax
from jax.experimental import pallas as pl
from jax.experimental.pallas import tpu as pltpu

F32 = jnp.float32
BF16 = jnp.bfloat16

CHUNK = 64
N_ADA = 9
MACARON_WEIGHT = 0.5
EPS = 1e-6
MASK_VALUE = -1e30
TINY = 1e-30
LANES = 128
SUBLANES = 8
MXU_DIM = 256
SUB = 4
PAD = SUBLANES
VMEM_LIMIT = 56 * 1024 * 1024
ROW_TILE = 512
INPROJ_TILE = 256
FFN_TILE = 1024
HGRN_TILE = 4096
Q_TILE = 512
KV_TILE = 2048
KV_STEPS_PER_TRIP = 2

_NT = (((1,), (1,)), ((), ()))
_TN = (((0,), (0,)), ((), ()))


def _params(*sem):
    return pltpu.CompilerParams(dimension_semantics=sem, vmem_limit_bytes=VMEM_LIMIT)


def _resident(shape):
    nd = len(shape)
    return pl.BlockSpec(shape, lambda *_: (0,) * nd, pipeline_mode=pl.Buffered(1))


def _modnorm(x, g, sc, sh):
    y = x * lax.rsqrt(jnp.mean(x * x, axis=-1, keepdims=True) + EPS) * g
    return y * (1.0 + sc) + sh


def _silu(x):
    return x * jax.nn.sigmoid(x)


def _mod_spec(per_row, tm, d, rows_per_seq):
    if per_row:
        return pl.BlockSpec((tm, d), lambda i, *_: (i, 0))
    tiles = rows_per_seq // tm
    return pl.BlockSpec((None, 1, d), lambda i, *_: (i // tiles, 0, 0))


def _ada_kernel(c_ref, w_ref, b_ref, o_ref):
    c = c_ref[...]
    o_ref[...] = jnp.dot(_silu(c).astype(BF16), w_ref[...].astype(BF16),
                         preferred_element_type=F32) + b_ref[...]


def _ada(c_all, w_ada, b_ada):
    depth, d, _ = w_ada.shape
    ns = c_all.shape[0]
    wide = 3 * d
    return pl.pallas_call(
        _ada_kernel,
        grid=(depth, N_ADA * d // wide),
        in_specs=[pl.BlockSpec((ns, d), lambda l, j: (0, 0)),
                  pl.BlockSpec((None, d, wide), lambda l, j: (l, 0, j)),
                  pl.BlockSpec((None, 1, wide), lambda l, j: (l, 0, j))],
        out_specs=pl.BlockSpec((None, ns, wide), lambda l, j: (l, 0, j)),
        out_shape=jax.ShapeDtypeStruct((depth, ns, N_ADA * d), F32),
        compiler_params=_params("parallel", "parallel"),
        name="ada",
    )(c_all, w_ada, b_ada.reshape(depth, 1, N_ADA * d))


def _ffn_kernel(x_ref, g_ref, sh_ref, sc_ref, gt_ref, wg_ref, wu_ref, wd_ref, gf_ref, o_ref,
                h_scr, acc_scr, *, n_chunks, final_norm):
    h_scr[...] = _modnorm(x_ref[...], g_ref[...], sc_ref[...], sh_ref[...]).astype(BF16)
    for c in range(n_chunks):
        h = h_scr[...]
        gate = jnp.dot(h, wg_ref[c], preferred_element_type=F32)
        up = jnp.dot(h, wu_ref[c], preferred_element_type=F32)
        a = (_silu(gate) * up).astype(BF16)
        down = jnp.dot(a, wd_ref[c], preferred_element_type=F32)
        if c == 0:
            acc_scr[...] = down
        else:
            acc_scr[...] += down
    y = x_ref[...] + MACARON_WEIGHT * gt_ref[...] * acc_scr[...]
    if final_norm:
        y = y * lax.rsqrt(jnp.mean(y * y, axis=-1, keepdims=True) + EPS) * gf_ref[...]
    o_ref[...] = y


def _ffn(x, g, sh, sc, gt, wg, wu, wd, g_final, *, tm, per_row, rows_per_seq, final_norm):
    t, d = x.shape
    n_chunks = wg.shape[0]
    mod = _mod_spec(per_row, tm, d, rows_per_seq)
    row = pl.BlockSpec((tm, d), lambda i: (i, 0))
    gain = pl.BlockSpec((1, d), lambda i: (0, 0))
    return pl.pallas_call(
        functools.partial(_ffn_kernel, n_chunks=n_chunks, final_norm=final_norm),
        grid=(t // tm,),
        in_specs=[row, gain, mod, mod, mod,
                  _resident(wg.shape), _resident(wu.shape), _resident(wd.shape), gain],
        out_specs=row,
        out_shape=jax.ShapeDtypeStruct((t, d), F32),
        scratch_shapes=[pltpu.VMEM((tm, d), BF16), pltpu.VMEM((tm, d), F32)],
        compiler_params=_params("parallel"),
        name="ffn",
    )(x, g, sh, sc, gt, wg, wu, wd, g_final)


N_QKV = 3


def _inproj_kernel(x_ref, g_ref, sh_ref, sc_ref, w_ref, kin_ref, vin_ref,
                   z_ref, qkv_ref, k_ref, v_ref, u_scr, *, q_scale):
    del kin_ref, vin_ref
    d = u_scr.shape[1]

    def group(g):
        return jnp.dot(u_scr[...], w_ref[g], preferred_element_type=F32)

    def cache_rows(ref, z):
        tm, nh = z.shape[0], z.shape[1] // LANES
        for h in range(nh):
            ref[pl.ds(h, tm, stride=nh), :] = z[:, h * LANES:(h + 1) * LANES]

    u_scr[...] = _modnorm(x_ref[...], g_ref[...], sc_ref[...], sh_ref[...]).astype(BF16)
    qkv_ref[:, 0:d] = (group(0) * q_scale).astype(BF16)
    for g, ref in ((1, k_ref), (2, v_ref)):
        z = group(g)
        cache_rows(ref, z)
        qkv_ref[:, g * d:(g + 1) * d] = z.astype(BF16)
    for g in range(N_QKV, w_ref.shape[0]):
        z_ref[:, (g - N_QKV) * d:(g - N_QKV + 1) * d] = group(g)


def _inproj(x, g, sh, sc, w3, kst, vst, layer, *, tm, per_row, rows_per_seq, q_scale):
    t, d = x.shape
    ng = w3.shape[0]
    mod = _mod_spec(per_row, tm, d, rows_per_seq)
    row = pl.BlockSpec((tm, d), lambda i: (i, 0))
    stack = pl.BlockSpec((None, tm * (d // LANES), LANES), lambda i: (layer, i, 0))
    anyspec = pl.BlockSpec(memory_space=pl.ANY)
    return pl.pallas_call(
        functools.partial(_inproj_kernel, q_scale=q_scale),
        grid=(t // tm,),
        in_specs=[row, pl.BlockSpec((1, d), lambda i: (0, 0)), mod, mod,
                  _resident(w3.shape), anyspec, anyspec],
        out_specs=[pl.BlockSpec((tm, (ng - N_QKV) * d), lambda i: (i, 0)),
                   pl.BlockSpec((tm, N_QKV * d), lambda i: (i, 0)),
                   stack, stack],
        out_shape=[jax.ShapeDtypeStruct((t, (ng - N_QKV) * d), F32),
                   jax.ShapeDtypeStruct((t, N_QKV * d), BF16),
                   jax.ShapeDtypeStruct(kst.shape, F32),
                   jax.ShapeDtypeStruct(vst.shape, F32)],
        scratch_shapes=[pltpu.VMEM((tm, d), BF16)],
        input_output_aliases={5: 2, 6: 3},
        compiler_params=_params("parallel"),
        name="inproj",
    )(x, g, sh, sc, w3, kst, vst)


def _lambda(lamp_ref, lam_init):
    lp = lamp_ref[...]
    s01 = jnp.sum(lp[0:1] * lp[1:2], axis=-1, keepdims=True)
    s23 = jnp.sum(lp[2:3] * lp[3:4], axis=-1, keepdims=True)
    return jnp.exp(s01) - jnp.exp(s23) + lam_init


def _split_maps(q, hd):
    lane = lax.broadcasted_iota(jnp.int32, q.shape, 1)
    zero = jnp.zeros_like(q)
    return jnp.concatenate([jnp.where(lane < hd, q, zero), jnp.where(lane >= hd, q, zero)],
                           axis=0).astype(BF16)


def _visible(rows, cols, tq, q0, k0):
    r = lax.broadcasted_iota(jnp.int32, (rows, cols), 0)
    qpos = q0 + jnp.where(r >= tq, r - tq, r)
    kpos = k0 + lax.broadcasted_iota(jnp.int32, (rows, cols), 1)
    return kpos < (qpos // CHUNK + 1) * CHUNK


def _attn_finish(o1, o2, lam, lam_init, gsub):
    o = o1 - lam * o2
    y = o * lax.rsqrt(jnp.mean(o * o, axis=-1, keepdims=True) + EPS) * gsub
    return (y * (1.0 - lam_init)).astype(BF16)


def _attn_kernel(cst_ref, lamp_ref, gsub_ref, bias_ref, q_ref, k_ref, v_ref, o_ref,
                 qs_scr, vx_scr, m_scr, acc_scr, *, tq, kw, hd):
    i = pl.program_id(2)
    lam_init = cst_ref[0]

    @pl.when(i == 0)
    def _():
        vx_scr[:, :LANES] = v_ref[...]
        vx_scr[:, LANES:] = jnp.ones((vx_scr.shape[0], LANES), BF16)

    q = q_ref[...]
    lane = lax.broadcasted_iota(jnp.int32, q.shape, 1)
    qs_scr[:tq] = jnp.where(lane < hd, q, jnp.zeros_like(q))
    qs_scr[tq:] = jnp.where(lane >= hd, q, jnp.zeros_like(q))

    def kv_step(start, width, masked, first=False):
        k = k_ref[pl.ds(start, width), :]
        vx = vx_scr[pl.ds(start, width), :]
        maps = (slice(0, tq), slice(tq, 2 * tq))
        scores = [lax.dot_general(qs_scr[rows], k, _NT, preferred_element_type=F32)
                  for rows in maps]
        for rows, s in zip(maps, scores):
            if masked:
                diag = s[:, width - tq:] + bias_ref[...]
                s = diag if width == tq else jnp.concatenate([s[:, :width - tq], diag], axis=1)
            m_cur = jnp.max(s, axis=-1, keepdims=True)
            m_new = jnp.broadcast_to(m_cur, (tq, LANES)) if first else jnp.maximum(m_scr[rows], m_cur)
            p = jnp.exp2(s - jnp.tile(m_new, (1, width // LANES))).astype(BF16)
            pv = jnp.dot(p, vx, preferred_element_type=F32)
            if first:
                acc_scr[rows] = pv
            else:
                alpha = jnp.exp2(m_scr[rows] - m_new)
                acc_scr[rows] = jnp.tile(alpha, (1, 2)) * acc_scr[rows] + pv
            m_scr[rows] = m_new

    def full_step(j):
        kv_step(pl.multiple_of(j * kw, kw), kw, False)

    n_full = (i * tq) // kw

    for pieces in range(kw // tq):
        for left in range(KV_STEPS_PER_TRIP):
            @pl.when((i % (kw // tq) == pieces) & (n_full % KV_STEPS_PER_TRIP == left))
            def _():
                kv_step(pl.multiple_of((i - pieces) * tq, tq), (pieces + 1) * tq, True, first=True)
                for s in range(left):
                    full_step(n_full - left + s)

    def full_body(k, carry):
        for s in range(KV_STEPS_PER_TRIP):
            full_step(KV_STEPS_PER_TRIP * k + s)
        return carry

    lax.fori_loop(0, n_full // KV_STEPS_PER_TRIP, full_body, 0)

    acc = acc_scr[...]
    o12 = acc[:, :LANES] / acc[:, LANES:]
    lam = _lambda(lamp_ref, lam_init)
    o_ref[...] = _attn_finish(o12[:tq], o12[tq:], lam, lam_init, gsub_ref[...])


def _attn(cst, lamp, gsub, qkv, *, nb, t, nh, tq):
    bt, d = qkv.shape[0], nh * LANES
    nq = t // tq
    kw = KV_TILE if KV_TILE % tq == 0 and KV_TILE <= t else tq
    assert tq % CHUNK == 0 or nq == 1, (tq, CHUNK)
    chunk = jnp.arange(tq) // CHUNK
    bias = jnp.where(chunk[None, :] <= chunk[:, None], 0.0, MASK_VALUE).astype(F32)

    def kv(group):
        return pl.BlockSpec((t, LANES), lambda b, h, i: (b, group * nh + h))

    return pl.pallas_call(
        functools.partial(_attn_kernel, tq=tq, kw=kw, hd=LANES // 2),
        grid=(nb, nh, nq),
        in_specs=[pl.BlockSpec(memory_space=pltpu.SMEM),
                  pl.BlockSpec(lamp.shape, lambda b, h, i: (0, 0)),
                  pl.BlockSpec((1, LANES), lambda b, h, i: (0, 0)),
                  _resident(bias.shape),
                  pl.BlockSpec((tq, LANES), lambda b, h, i: (b * nq + i, h)),
                  kv(1), kv(2)],
        out_specs=pl.BlockSpec((tq, LANES), lambda b, h, i: (b * nq + i, h)),
        out_shape=jax.ShapeDtypeStruct((bt, d), BF16),
        scratch_shapes=[pltpu.VMEM((2 * tq, LANES), BF16), pltpu.VMEM((t, 2 * LANES), BF16),
                        pltpu.VMEM((2 * tq, LANES), F32), pltpu.VMEM((2 * tq, 2 * LANES), F32)],
        compiler_params=_params("parallel", "parallel", "arbitrary"),
        name="attn",
    )(cst, lamp, gsub, bias, qkv, qkv, qkv)


def _attn_cached_kernel(cst_ref, lamp_ref, gsub_ref, q_ref, kc_ref, vc_ref, kn_ref, vn_ref, o_ref,
                        *, tq, past, hd):
    lam_init = cst_ref[0]
    lam = _lambda(lamp_ref, lam_init)
    nh = q_ref.shape[1] // LANES
    masked = past % CHUNK + tq > CHUNK
    for h in range(nh):
        cols = slice(h * LANES, (h + 1) * LANES)
        head = pl.ds(h, past, stride=nh)
        qs = _split_maps(q_ref[:, cols], hd)

        def scores(k, k0):
            s = lax.dot_general(qs, k.astype(BF16), _NT, preferred_element_type=F32)
            if masked:
                s = jnp.where(_visible(2 * tq, k.shape[0], tq, past, k0), s, MASK_VALUE)
            return s

        sc = scores(kc_ref[head, :], 0)
        sn = scores(kn_ref[:, cols], past)
        m = jnp.maximum(jnp.max(sc, axis=-1, keepdims=True), jnp.max(sn, axis=-1, keepdims=True))
        pc = jnp.exp2(sc - m)
        pn = jnp.exp2(sn - m)
        l = jnp.sum(pc, axis=-1, keepdims=True) + jnp.sum(pn, axis=-1, keepdims=True)
        acc = (jnp.dot(pc.astype(BF16), vc_ref[head, :].astype(BF16), preferred_element_type=F32)
               + jnp.dot(pn.astype(BF16), vn_ref[:, cols], preferred_element_type=F32))
        o12 = acc / l
        o_ref[:, cols] = _attn_finish(o12[:tq], o12[tq:], lam, lam_init, gsub_ref[...])


def _attn_cached(cst, lamp, gsub, qkv, ck, cv, layer, *, nb, t, nh):
    bt, d = qkv.shape[0], nh * LANES
    past = ck.shape[2]
    ck, cv = (c.reshape(c.shape[0], nb, past * nh, LANES) for c in (ck, cv))
    cache = pl.BlockSpec((None, None, past * nh, LANES), lambda b: (layer, b, 0, 0))

    def rows(group):
        return pl.BlockSpec((t, d), lambda b: (b, group))

    return pl.pallas_call(
        functools.partial(_attn_cached_kernel, tq=t, past=past, hd=LANES // 2),
        grid=(nb,),
        in_specs=[pl.BlockSpec(memory_space=pltpu.SMEM),
                  pl.BlockSpec(lamp.shape, lambda b: (0, 0)),
                  pl.BlockSpec((1, LANES), lambda b: (0, 0)),
                  rows(0), cache, cache, rows(1), rows(2)],
        out_specs=pl.BlockSpec((t, d), lambda b: (b, 0)),
        out_shape=jax.ShapeDtypeStruct((bt, d), BF16),
        compiler_params=_params("parallel"),
        name="attn_cached",
    )(cst, lamp, gsub, qkv, ck, cv, qkv, qkv)


def _split3(x):
    hi = x.astype(BF16)
    r = x - hi.astype(F32)
    mid = r.astype(BF16)
    lo = (r - mid.astype(F32)).astype(BF16)
    return hi, mid, lo


def _hgrn_kernel(*refs, heads, tt, blk, has_s0):
    n_in = 10 if has_s0 else 9
    for h in range(heads):
        cols = pl.ds(h * LANES, LANES)
        per_head = [r.at[:, cols] for r in refs[:5]] + list(refs[5:9])
        if has_s0:
            per_head.append(refs[9].at[h])
        o_ref, sf_ref, st_scr, pad_scr = refs[n_in:]
        _hgrn_head(*per_head, o_ref.at[:, cols], sf_ref.at[h], st_scr.at[h], pad_scr.at[h],
                   tt=tt, blk=blk, has_s0=has_s0)


def _hgrn_head(*refs, tt, blk, has_s0):
    hq_ref, hf_ref, hi_ref, hg_ref, lb_ref, gn_ref, tri_ref, ones2_ref, lvl_ref = refs[:9]
    s0_ref = refs[9] if has_s0 else None
    o_ref, sf_ref, st_scr, pad_scr = refs[-4:]
    t = pl.program_id(2)

    @pl.when(t == 0)
    def _():
        if has_s0:
            st_scr[...] = s0_ref[...].T
        else:
            st_scr[...] = jnp.zeros_like(st_scr)
        pad_scr[:, 0:PAD, :] = jnp.zeros((3, PAD, LANES), F32)

    lb = lb_ref[...]
    sig = jax.nn.sigmoid(hf_ref[...])
    logf = jnp.log(jnp.maximum(lb + (1.0 - lb) * sig, TINY)) * math.log2(math.e)
    kk = (1.0 - lb) * (1.0 - sig)
    qq = _silu(hq_ref[...])
    vv = hi_ref[...]

    tri = tri_ref[...]
    parts = _split3(logf)
    b = jnp.concatenate(
        [sum(jnp.dot(tri, part[c * blk:(c + 1) * blk], preferred_element_type=F32)
             for part in parts) for c in range(tt // blk)], axis=0)

    pad_scr[0, PAD:, :] = kk
    pad_scr[1, PAD:, :] = b
    pad_scr[2, PAD:, :] = vv
    ones2 = ones2_ref[...]
    pos = lax.broadcasted_iota(jnp.int32, (PAD, LANES), 0) % SUB

    def lag_terms(d):
        if d == 0:
            return (qq * kk).astype(BF16), vv
        ks, bs, vs = (pad_scr[r, PAD - d:PAD - d + tt, :] for r in range(3))
        off = jnp.tile(jnp.where(pos >= d, 0.0, MASK_VALUE), (tt // PAD, 1))
        return (qq * ks * jnp.exp2(b - bs + off)).astype(BF16), vs

    o = jnp.zeros((tt, LANES), F32)
    for d in range(0, SUB, 2):
        (p0, v0), (p1, v1) = lag_terms(d), lag_terms(d + 1)
        r = jnp.dot(jnp.concatenate([p0, p1], axis=1), ones2, preferred_element_type=F32)
        o = o + r[:, :LANES] * v0 + r[:, LANES:] * v1

    st = st_scr[...]
    older = []
    for c in range(tt // blk):
        sl = slice(c * blk, (c + 1) * blk)
        bc = b[sl]
        b_last = bc[blk - 1:blk]
        qt = (qq[sl] * jnp.exp2(bc)).astype(BF16)
        older.append(lax.dot_general(qt, st.astype(BF16), _NT, preferred_element_type=F32))
        kt = (kk[sl] * jnp.exp2(b_last - bc)).astype(BF16)
        st = st * jnp.exp2(b_last) + lax.dot_general(vv[sl].astype(BF16), kt, _TN,
                                                    preferred_element_type=F32)
    st_scr[...] = st
    o = o + jnp.concatenate(older, axis=0)

    weights = [None] * (tt // blk)
    m, level = SUB, 0
    while 2 * m <= blk:
        b3 = b.reshape(tt // (2 * m), 2 * m, LANES)
        w = jnp.exp2(-jnp.abs(b3 - b3[:, m - 1:m, :])).reshape(tt, LANES)
        qt = (qq * w).astype(BF16)
        kt = (kk * w).astype(BF16)
        for c in range(tt // blk):
            sl = slice(c * blk, (c + 1) * blk)
            a = lax.dot_general(qt[sl], kt[sl], _NT, preferred_element_type=F32) * lvl_ref[level]
            weights[c] = a if weights[c] is None else weights[c] + a
        m, level = 2 * m, level + 1
    if level:
        o = o + jnp.concatenate(
            [jnp.dot(weights[c].astype(BF16), vv[c * blk:(c + 1) * blk].astype(BF16),
                     preferred_element_type=F32) for c in range(tt // blk)], axis=0)

    y = o * lax.rsqrt(jnp.mean(o * o, axis=-1, keepdims=True) + EPS) * gn_ref[...]
    o_ref[...] = (y * _silu(hg_ref[...])).astype(BF16)

    @pl.when(t == pl.num_programs(2) - 1)
    def _():
        sf_ref[...] = st.T


def _hgrn(z, lb, gn, s0, layer, *, nb, t, nh, tt, blk, heads):
    bt = z.shape[0]
    nt = t // tt
    has_s0 = s0 is not None
    ng = nh // heads
    wide = heads * LANES

    def col(group):
        return pl.BlockSpec((tt, wide), lambda b, h, i: (b * nt + i, group * ng + h))

    state = pl.BlockSpec((None, heads, LANES, LANES), lambda b, h, i: (b, h, 0, 0))
    tri = jnp.tril(jnp.ones((blk, blk), BF16))
    ones2 = jnp.kron(jnp.eye(2, dtype=BF16), jnp.ones((LANES, LANES), BF16))
    pos = jnp.arange(blk)
    sizes = [SUB << i for i in range(max(blk // SUB, 1).bit_length() - 1)]
    lvl = jnp.stack([((pos[:, None] // (2 * m) == pos[None, :] // (2 * m))
                      & (pos[:, None] % (2 * m) >= m) & (pos[None, :] % (2 * m) < m))
                     for m in sizes]).astype(F32) if sizes else jnp.zeros((1, blk, blk), F32)
    in_specs = [col(0), col(1), col(2), col(3),
                pl.BlockSpec((1, wide), lambda b, h, i: (0, h)),
                pl.BlockSpec((1, LANES), lambda b, h, i: (0, 0)),
                _resident(tri.shape), _resident(ones2.shape), _resident(lvl.shape)]
    args = [z, z, z, z, lb, gn, tri, ones2, lvl]
    if has_s0:
        in_specs.append(pl.BlockSpec((None, None, heads, LANES, LANES),
                                     lambda b, h, i: (layer, b, h, 0, 0)))
        args.append(s0)
    return pl.pallas_call(
        functools.partial(_hgrn_kernel, heads=heads, tt=tt, blk=blk, has_s0=has_s0),
        grid=(nb, ng, nt),
        in_specs=in_specs,
        out_specs=[pl.BlockSpec((tt, wide), lambda b, h, i: (b * nt + i, h)), state],
        out_shape=[jax.ShapeDtypeStruct((bt, nh * LANES), BF16),
                   jax.ShapeDtypeStruct((nb, nh, LANES, LANES), F32)],
        scratch_shapes=[pltpu.VMEM((heads, LANES, LANES), F32),
                        pltpu.VMEM((heads, 3, PAD + tt, LANES), F32)],
        compiler_params=_params("parallel", "parallel", "arbitrary"),
        name="hgrn",
    )(*args)


def _merge_kernel(x_ref, gt_ref, oa_ref, oh_ref, za_ref, zh_ref, wa_ref, wh_ref, wo_ref, o_ref):
    a = jnp.dot(oa_ref[...], wa_ref[...], preferred_element_type=F32)
    r = jnp.dot(oh_ref[...], wh_ref[...], preferred_element_type=F32)
    merged = jax.nn.sigmoid(za_ref[...]) * a + jax.nn.sigmoid(zh_ref[...]) * r
    o_ref[...] = x_ref[...] + gt_ref[...] * jnp.dot(merged.astype(BF16), wo_ref[...],
                                                    preferred_element_type=F32)


def _merge(x, gt, oa, oh, z, wa, wh, wo, *, tm, per_row, rows_per_seq):
    t, d = x.shape
    mod = _mod_spec(per_row, tm, d, rows_per_seq)
    row = pl.BlockSpec((tm, d), lambda i: (i, 0))
    return pl.pallas_call(
        _merge_kernel,
        grid=(t // tm,),
        in_specs=[row, mod, row, row,
                  pl.BlockSpec((tm, d), lambda i: (i, 4)), pl.BlockSpec((tm, d), lambda i: (i, 5)),
                  _resident(wa.shape), _resident(wh.shape), _resident(wo.shape)],
        out_specs=row,
        out_shape=jax.ShapeDtypeStruct((t, d), F32),
        compiler_params=_params("parallel"),
        name="merge",
    )(x, gt, oa, oh, z, z, wa, wh, wo)


def _tile(n, target):
    if n <= target:
        return n
    for c in range(target, SUBLANES - 1, -1):
        if n % c == 0 and c % SUBLANES == 0:
            return c
    return n


def _layer_weights(p, l, d, fc):
    f = p["w_ffn1_d"].shape[1]
    nc = f // fc

    def ffn(w_gu, w_d):
        wg = w_gu[l, :, :f].reshape(d, nc, fc).transpose(1, 0, 2).astype(BF16)
        wu = w_gu[l, :, f:].reshape(d, nc, fc).transpose(1, 0, 2).astype(BF16)
        return wg, wu, w_d[l].reshape(nc, fc, d).astype(BF16)

    ng = p["w_in"].shape[2] // d
    return dict(
        ffn1=ffn(p["w_ffn1_gu"], p["w_ffn1_d"]),
        ffn2=ffn(p["w_ffn2_gu"], p["w_ffn2_d"]),
        w_in=p["w_in"][l].reshape(d, ng, d).transpose(1, 0, 2).astype(BF16),
        w_ba=p["w_br_att"][l].astype(BF16),
        w_bh=p["w_br_hg"][l].astype(BF16),
        w_out=p["w_out"][l].astype(BF16),
    )


def _trunk(x3, mods, p, lb_all, weights, cache):
    nb, t, d = x3.shape
    depth = p["w_ada"].shape[0]
    nh = d // LANES
    bt = nb * t
    x = x3.reshape(bt, d)
    per_row = t < INPROJ_TILE
    tm = _tile(bt if per_row else t, ROW_TILE)
    kst = jnp.zeros((depth, bt * nh, LANES), F32)
    vst = jnp.zeros((depth, bt * nh, LANES), F32)
    states = []
    for l in range(depth):
        w = weights[l]
        m = mods[l]
        if per_row:
            mm = [jnp.repeat(m[:, i], t, axis=0) for i in range(N_ADA)]
        else:
            mm = [m[:, i].reshape(nb, 1, d) for i in range(N_ADA)]
        sh1, sc1, g1, sh2, sc2, g2, sh3, sc3, g3 = mm
        kw = dict(tm=tm, per_row=per_row, rows_per_seq=t)
        row = lambda a: a.reshape(1, -1)

        g_final = row(p["g_final"])
        kw_ffn = dict(kw, tm=_tile(bt if per_row else t, FFN_TILE))
        x = _ffn(x, row(p["g_ffn1"][l]), sh1, sc1, g1, *w["ffn1"], g_final, final_norm=False,
                 **kw_ffn)
        q_scale = (LANES // 2) ** -0.5 * math.log2(math.e)
        kw_in = dict(kw, tm=_tile(tm, INPROJ_TILE))
        z, qkv, kst, vst = _inproj(x, row(p["g_mix"][l]), sh2, sc2, w["w_in"], kst, vst, l,
                                   q_scale=q_scale, **kw_in)

        lam_init = 0.8 - 0.6 * math.exp(-0.3 * l)
        cst = jnp.array([lam_init], F32)
        gsub = row(p["g_att_sub"][l])
        if cache is None:
            oa = _attn(cst, p["att_lambda"][l], gsub, qkv, nb=nb, t=t, nh=nh,
                       tq=_tile(t, Q_TILE))
            s0 = None
        else:
            ck, cv, s0 = cache
            oa = _attn_cached(cst, p["att_lambda"][l], gsub, qkv, ck, cv, l, nb=nb, t=t, nh=nh)
        tt = _tile(t, HGRN_TILE)
        blk = min(tt, 128)
        oh, s_fin = _hgrn(z, row(lb_all[l]), row(p["g_hg_norm"][l]), s0, l,
                          nb=nb, t=t, nh=nh, tt=tt, blk=blk, heads=nh if per_row else 1)
        states.append(s_fin)

        x = _merge(x, g2, oa, oh, z, w["w_ba"], w["w_bh"], w["w_out"], **kw)
        x = _ffn(x, row(p["g_ffn2"][l]), sh3, sc3, g3, *w["ffn2"], g_final,
                 final_norm=l == depth - 1, **kw_ffn)

    y = x.reshape(nb, t, d)
    shape5 = (depth, nb, t, nh, LANES)
    return y, kst.reshape(shape5), vst.reshape(shape5), jnp.stack(states)


def kernel(x_prompt, x_sample, cache_k, cache_v, state_hgrn, c_prompt, c_sample, w_ada, b_ada, g_ffn1, w_ffn1_gu, w_ffn1_d, g_mix, w_in, att_lambda, g_att_sub, hg_lb_logits, g_hg_norm, w_br_att, w_br_hg, w_out, g_ffn2, w_ffn2_gu, w_ffn2_d, g_final):
    p = dict(w_ada=w_ada, b_ada=b_ada, g_ffn1=g_ffn1, w_ffn1_gu=w_ffn1_gu, w_ffn1_d=w_ffn1_d,
             g_mix=g_mix, w_in=w_in, att_lambda=att_lambda, g_att_sub=g_att_sub,
             g_hg_norm=g_hg_norm, w_br_att=w_br_att, w_br_hg=w_br_hg, w_out=w_out,
             g_ffn2=g_ffn2, w_ffn2_gu=w_ffn2_gu, w_ffn2_d=w_ffn2_d, g_final=g_final)
    depth, d, _ = w_ada.shape
    nbp, nbs = x_prompt.shape[0], x_sample.shape[0]

    lb_sm = jax.nn.softmax(hg_lb_logits.astype(F32), axis=0)
    lb_all = jnp.cumsum(lb_sm, axis=0) - lb_sm[0]

    ns = nbp + nbs
    nsp = -(-ns // SUBLANES) * SUBLANES
    c_all = jnp.concatenate([c_prompt, c_sample, jnp.zeros((nsp - ns, d), F32)], axis=0)
    mods = _ada(c_all, w_ada, b_ada).reshape(depth, nsp, N_ADA, d)

    f = w_ffn1_d.shape[1]
    fc = MXU_DIM if f % MXU_DIM == 0 else LANES
    weights = [_layer_weights(p, l, d, fc) for l in range(depth)]

    y_p, k_p, v_p, s_p = _trunk(x_prompt, mods[:, :nbp], p, lb_all, weights, None)
    y_s, k_s, v_s, s_s = _trunk(x_sample, mods[:, nbp:ns], p, lb_all, weights,
                                (cache_k, cache_v, state_hgrn))
    return (y_p, y_s, k_p, v_p, s_p, k_s, v_s, s_s)
```

```python
import functools
import math

import jax
import jax.numpy as jnp
from jax import lax
from jax.experimental import pallas as pl
from jax.experimental.pallas import tpu as pltpu

F32 = jnp.float32
BF16 = jnp.bfloat16

CHUNK = 64
N_ADA = 9
MACARON_WEIGHT = 0.5
EPS = 1e-6
MASK_VALUE = -1e30
TINY = 1e-30
LANES = 128
SUBLANES = 8
MXU_DIM = 256
SUB = 4
PAD = SUBLANES
VMEM_LIMIT = 56 * 1024 * 1024
ROW_TILE = 512
INPROJ_TILE = 256
FFN_TILE = 1024
HGRN_TILE = 4096
Q_TILE = 512
KV_TILE = 2048
KV_STEPS_PER_TRIP = 2

_NT = (((1,), (1,)), ((), ()))
_TN = (((0,), (0,)), ((), ()))


def _params(*sem):
    return pltpu.CompilerParams(dimension_semantics=sem, vmem_limit_bytes=VMEM_LIMIT)


def _resident(shape):
    nd = len(shape)
    return pl.BlockSpec(shape, lambda *_: (0,) * nd, pipeline_mode=pl.Buffered(1))


def _modnorm(x, g, sc, sh):
    y = x * lax.rsqrt(jnp.mean(x * x, axis=-1, keepdims=True) + EPS) * g
    return y * (1.0 + sc) + sh


def _silu(x):
    return x * jax.nn.sigmoid(x)


def _mod_spec(per_row, tm, d, rows_per_seq):
    if per_row:
        return pl.BlockSpec((tm, d), lambda i, *_: (i, 0))
    tiles = rows_per_seq // tm
    return pl.BlockSpec((None, 1, d), lambda i, *_: (i // tiles, 0, 0))


def _ada_kernel(c_ref, w_ref, b_ref, o_ref):
    c = c_ref[...]
    o_ref[...] = jnp.dot(_silu(c).astype(BF16), w_ref[...].astype(BF16),
                         preferred_element_type=F32) + b_ref[...]


def _ada(c_all, w_ada, b_ada):
    depth, d, _ = w_ada.shape
    ns = c_all.shape[0]
    wide = 3 * d
    return pl.pallas_call(
        _ada_kernel,
        grid=(depth, N_ADA * d // wide),
        in_specs=[pl.BlockSpec((ns, d), lambda l, j: (0, 0)),
                  pl.BlockSpec((None, d, wide), lambda l, j: (l, 0, j)),
                  pl.BlockSpec((None, 1, wide), lambda l, j: (l, 0, j))],
        out_specs=pl.BlockSpec((None, ns, wide), lambda l, j: (l, 0, j)),
        out_shape=jax.ShapeDtypeStruct((depth, ns, N_ADA * d), F32),
        compiler_params=_params("parallel", "parallel"),
        name="ada",
    )(c_all, w_ada, b_ada.reshape(depth, 1, N_ADA * d))


def _ffn_kernel(x_ref, g_ref, sh_ref, sc_ref, gt_ref, wg_ref, wu_ref, wd_ref, gf_ref, o_ref,
                h_scr, acc_scr, *, n_chunks, final_norm):
    h_scr[...] = _modnorm(x_ref[...], g_ref[...], sc_ref[...], sh_ref[...]).astype(BF16)
    for c in range(n_chunks):
        h = h_scr[...]
        gate = jnp.dot(h, wg_ref[c], preferred_element_type=F32)
        up = jnp.dot(h, wu_ref[c], preferred_element_type=F32)
        a = (_silu(gate) * up).astype(BF16)
        down = jnp.dot(a, wd_ref[c], preferred_element_type=F32)
        if c == 0:
            acc_scr[...] = down
        else:
            acc_scr[...] += down
    y = x_ref[...] + MACARON_WEIGHT * gt_ref[...] * acc_scr[...]
    if final_norm:
        y = y * lax.rsqrt(jnp.mean(y * y, axis=-1, keepdims=True) + EPS) * gf_ref[...]
    o_ref[...] = y


def _ffn(x, g, sh, sc, gt, wg, wu, wd, g_final, *, tm, per_row, rows_per_seq, final_norm):
    t, d = x.shape
    n_chunks = wg.shape[0]
    mod = _mod_spec(per_row, tm, d, rows_per_seq)
    row = pl.BlockSpec((tm, d), lambda i: (i, 0))
    gain = pl.BlockSpec((1, d), lambda i: (0, 0))
    return pl.pallas_call(
        functools.partial(_ffn_kernel, n_chunks=n_chunks, final_norm=final_norm),
        grid=(t // tm,),
        in_specs=[row, gain, mod, mod, mod,
                  _resident(wg.shape), _resident(wu.shape), _resident(wd.shape), gain],
        out_specs=row,
        out_shape=jax.ShapeDtypeStruct((t, d), F32),
        scratch_shapes=[pltpu.VMEM((tm, d), BF16), pltpu.VMEM((tm, d), F32)],
        compiler_params=_params("parallel"),
        name="ffn",
    )(x, g, sh, sc, gt, wg, wu, wd, g_final)


N_QKV = 3
N_GATES = 2


def _inproj_kernel(x_ref, g_ref, sh_ref, sc_ref, w_ref, kin_ref, vin_ref,
                   z_ref, qkv_ref, k_ref, v_ref, gates_ref, u_scr, *, q_scale):
    del kin_ref, vin_ref
    d = u_scr.shape[1]

    def group(g):
        return jnp.dot(u_scr[...], w_ref[g], preferred_element_type=F32)

    def cache_rows(ref, z):
        tm, nh = z.shape[0], z.shape[1] // LANES
        for h in range(nh):
            ref[pl.ds(h, tm, stride=nh), :] = z[:, h * LANES:(h + 1) * LANES]

    u_scr[...] = _modnorm(x_ref[...], g_ref[...], sc_ref[...], sh_ref[...]).astype(BF16)
    qkv_ref[:, 0:d] = (group(0) * q_scale).astype(BF16)
    for g, ref in ((1, k_ref), (2, v_ref)):
        z = group(g)
        cache_rows(ref, z)
        qkv_ref[:, g * d:(g + 1) * d] = z.astype(BF16)
    n_z = w_ref.shape[0] - N_QKV - N_GATES
    for g in range(n_z):
        z_ref[:, g * d:(g + 1) * d] = group(N_QKV + g)
    for g in range(N_GATES):
        gates_ref[:, g * d:(g + 1) * d] = jax.nn.sigmoid(group(N_QKV + n_z + g)).astype(BF16)


def _inproj(x, g, sh, sc, w3, kst, vst, layer, *, tm, per_row, rows_per_seq, q_scale):
    t, d = x.shape
    ng = w3.shape[0]
    mod = _mod_spec(per_row, tm, d, rows_per_seq)
    row = pl.BlockSpec((tm, d), lambda i: (i, 0))
    stack = pl.BlockSpec((None, tm * (d // LANES), LANES), lambda i: (layer, i, 0))
    anyspec = pl.BlockSpec(memory_space=pl.ANY)
    return pl.pallas_call(
        functools.partial(_inproj_kernel, q_scale=q_scale),
        grid=(t // tm,),
        in_specs=[row, pl.BlockSpec((1, d), lambda i: (0, 0)), mod, mod,
                  _resident(w3.shape), anyspec, anyspec],
        out_specs=[pl.BlockSpec((tm, (ng - N_QKV - N_GATES) * d), lambda i: (i, 0)),
                   pl.BlockSpec((tm, N_QKV * d), lambda i: (i, 0)),
                   stack, stack,
                   pl.BlockSpec((tm, N_GATES * d), lambda i: (i, 0))],
        out_shape=[jax.ShapeDtypeStruct((t, (ng - N_QKV - N_GATES) * d), F32),
                   jax.ShapeDtypeStruct((t, N_QKV * d), BF16),
                   jax.ShapeDtypeStruct(kst.shape, F32),
                   jax.ShapeDtypeStruct(vst.shape, F32),
                   jax.ShapeDtypeStruct((t, N_GATES * d), BF16)],
        scratch_shapes=[pltpu.VMEM((tm, d), BF16)],
        input_output_aliases={5: 2, 6: 3},
        compiler_params=_params("parallel"),
        name="inproj",
    )(x, g, sh, sc, w3, kst, vst)


def _lambda(lamp_ref, lam_init):
    lp = lamp_ref[...]
    s01 = jnp.sum(lp[0:1] * lp[1:2], axis=-1, keepdims=True)
    s23 = jnp.sum(lp[2:3] * lp[3:4], axis=-1, keepdims=True)
    return jnp.exp(s01) - jnp.exp(s23) + lam_init


def _split_maps(q, hd):
    lane = lax.broadcasted_iota(jnp.int32, q.shape, 1)
    zero = jnp.zeros_like(q)
    return jnp.concatenate([jnp.where(lane < hd, q, zero), jnp.where(lane >= hd, q, zero)],
                           axis=0).astype(BF16)


def _visible(rows, cols, tq, q0, k0):
    r = lax.broadcasted_iota(jnp.int32, (rows, cols), 0)
    qpos = q0 + jnp.where(r >= tq, r - tq, r)
    kpos = k0 + lax.broadcasted_iota(jnp.int32, (rows, cols), 1)
    return kpos < (qpos // CHUNK + 1) * CHUNK


def _attn_finish(o1, o2, lam, lam_init, gsub):
    o = o1 - lam * o2
    y = o * lax.rsqrt(jnp.mean(o * o, axis=-1, keepdims=True) + EPS) * gsub
    return (y * (1.0 - lam_init)).astype(BF16)


def _attn_kernel(cst_ref, lamp_ref, gsub_ref, bias_ref, q_ref, k_ref, v_ref, o_ref,
                 qs_scr, vx_scr, m_scr, acc_scr, *, tq, kw, hd):
    i = pl.program_id(2)
    lam_init = cst_ref[0]

    @pl.when(i == 0)
    def _():
        vx_scr[:, :LANES] = v_ref[...]
        vx_scr[:, LANES:] = jnp.ones((vx_scr.shape[0], LANES), BF16)

    q = q_ref[...]
    lane = lax.broadcasted_iota(jnp.int32, q.shape, 1)
    qs_scr[:tq] = jnp.where(lane < hd, q, jnp.zeros_like(q))
    qs_scr[tq:] = jnp.where(lane >= hd, q, jnp.zeros_like(q))

    def kv_step(start, width, masked, first=False):
        k = k_ref[pl.ds(start, width), :]
        vx = vx_scr[pl.ds(start, width), :]
        maps = (slice(0, tq), slice(tq, 2 * tq))
        scores = [lax.dot_general(qs_scr[rows], k, _NT, preferred_element_type=F32)
                  for rows in maps]
        for rows, s in zip(maps, scores):
            if masked:
                diag = s[:, width - tq:] + bias_ref[...]
                s = diag if width == tq else jnp.concatenate([s[:, :width - tq], diag], axis=1)
            m_cur = jnp.max(s, axis=-1, keepdims=True)
            m_new = jnp.broadcast_to(m_cur, (tq, LANES)) if first else jnp.maximum(m_scr[rows], m_cur)
            p = jnp.exp2(s - jnp.tile(m_new, (1, width // LANES))).astype(BF16)
            pv = jnp.dot(p, vx, preferred_element_type=F32)
            if first:
                acc_scr[rows] = pv
            else:
                alpha = jnp.exp2(m_scr[rows] - m_new)
                acc_scr[rows] = jnp.tile(alpha, (1, 2)) * acc_scr[rows] + pv
            m_scr[rows] = m_new

    def full_step(j):
        kv_step(pl.multiple_of(j * kw, kw), kw, False)

    n_full = (i * tq) // kw

    for pieces in range(kw // tq):
        for left in range(KV_STEPS_PER_TRIP):
            @pl.when((i % (kw // tq) == pieces) & (n_full % KV_STEPS_PER_TRIP == left))
            def _():
                kv_step(pl.multiple_of((i - pieces) * tq, tq), (pieces + 1) * tq, True, first=True)
                for s in range(left):
                    full_step(n_full - left + s)

    def full_body(k, carry):
        for s in range(KV_STEPS_PER_TRIP):
            full_step(KV_STEPS_PER_TRIP * k + s)
        return carry

    lax.fori_loop(0, n_full // KV_STEPS_PER_TRIP, full_body, 0)

    acc = acc_scr[...]
    o12 = acc[:, :LANES] / acc[:, LANES:]
    lam = _lambda(lamp_ref, lam_init)
    o_ref[...] = _attn_finish(o12[:tq], o12[tq:], lam, lam_init, gsub_ref[...])


def _attn(cst, lamp, gsub, qkv, *, nb, t, nh, tq):
    bt, d = qkv.shape[0], nh * LANES
    nq = t // tq
    kw = KV_TILE if KV_TILE % tq == 0 and KV_TILE <= t else tq
    assert tq % CHUNK == 0 or nq == 1, (tq, CHUNK)
    chunk = jnp.arange(tq) // CHUNK
    bias = jnp.where(chunk[None, :] <= chunk[:, None], 0.0, MASK_VALUE).astype(F32)

    def kv(group):
        return pl.BlockSpec((t, LANES), lambda b, h, i: (b, group * nh + h))

    return pl.pallas_call(
        functools.partial(_attn_kernel, tq=tq, kw=kw, hd=LANES // 2),
        grid=(nb, nh, nq),
        in_specs=[pl.BlockSpec(memory_space=pltpu.SMEM),
                  pl.BlockSpec(lamp.shape, lambda b, h, i: (0, 0)),
                  pl.BlockSpec((1, LANES), lambda b, h, i: (0, 0)),
                  _resident(bias.shape),
                  pl.BlockSpec((tq, LANES), lambda b, h, i: (b * nq + i, h)),
                  kv(1), kv(2)],
        out_specs=pl.BlockSpec((tq, LANES), lambda b, h, i: (b * nq + i, h)),
        out_shape=jax.ShapeDtypeStruct((bt, d), BF16),
        scratch_shapes=[pltpu.VMEM((2 * tq, LANES), BF16), pltpu.VMEM((t, 2 * LANES), BF16),
                        pltpu.VMEM((2 * tq, LANES), F32), pltpu.VMEM((2 * tq, 2 * LANES), F32)],
        compiler_params=_params("parallel", "parallel", "arbitrary"),
        name="attn",
    )(cst, lamp, gsub, bias, qkv, qkv, qkv)


def _attn_cached_kernel(cst_ref, lamp_ref, gsub_ref, q_ref, kc_ref, vc_ref, kn_ref, vn_ref, o_ref,
                        *, tq, past, hd):
    lam_init = cst_ref[0]
    lam = _lambda(lamp_ref, lam_init)
    nh = q_ref.shape[1] // LANES
    masked = past % CHUNK + tq > CHUNK
    for h in range(nh):
        cols = slice(h * LANES, (h + 1) * LANES)
        head = pl.ds(h, past, stride=nh)
        qs = _split_maps(q_ref[:, cols], hd)

        def scores(k, k0):
            s = lax.dot_general(qs, k.astype(BF16), _NT, preferred_element_type=F32)
            if masked:
                s = jnp.where(_visible(2 * tq, k.shape[0], tq, past, k0), s, MASK_VALUE)
            return s

        sc = scores(kc_ref[head, :], 0)
        sn = scores(kn_ref[:, cols], past)
        m = jnp.maximum(jnp.max(sc, axis=-1, keepdims=True), jnp.max(sn, axis=-1, keepdims=True))
        pc = jnp.exp2(sc - m)
        pn = jnp.exp2(sn - m)
        l = jnp.sum(pc, axis=-1, keepdims=True) + jnp.sum(pn, axis=-1, keepdims=True)
        acc = (jnp.dot(pc.astype(BF16), vc_ref[head, :].astype(BF16), preferred_element_type=F32)
               + jnp.dot(pn.astype(BF16), vn_ref[:, cols], preferred_element_type=F32))
        o12 = acc / l
        o_ref[:, cols] = _attn_finish(o12[:tq], o12[tq:], lam, lam_init, gsub_ref[...])


def _attn_cached(cst, lamp, gsub, qkv, ck, cv, layer, *, nb, t, nh):
    bt, d = qkv.shape[0], nh * LANES
    past = ck.shape[2]
    ck, cv = (c.reshape(c.shape[0], nb, past * nh, LANES) for c in (ck, cv))
    cache = pl.BlockSpec((None, None, past * nh, LANES), lambda b: (layer, b, 0, 0))

    def rows(group):
        return pl.BlockSpec((t, d), lambda b: (b, group))

    return pl.pallas_call(
        functools.partial(_attn_cached_kernel, tq=t, past=past, hd=LANES // 2),
        grid=(nb,),
        in_specs=[pl.BlockSpec(memory_space=pltpu.SMEM),
                  pl.BlockSpec(lamp.shape, lambda b: (0, 0)),
                  pl.BlockSpec((1, LANES), lambda b: (0, 0)),
                  rows(0), cache, cache, rows(1), rows(2)],
        out_specs=pl.BlockSpec((t, d), lambda b: (b, 0)),
        out_shape=jax.ShapeDtypeStruct((bt, d), BF16),
        compiler_params=_params("parallel"),
        name="attn_cached",
    )(cst, lamp, gsub, qkv, ck, cv, qkv, qkv)


def _split3(x):
    hi = x.astype(BF16)
    r = x - hi.astype(F32)
    mid = r.astype(BF16)
    lo = (r - mid.astype(F32)).astype(BF16)
    return hi, mid, lo


def _hgrn_kernel(*refs, heads, tt, blk, has_s0):
    n_in = 10 if has_s0 else 9
    for h in range(heads):
        cols = pl.ds(h * LANES, LANES)
        per_head = [r.at[:, cols] for r in refs[:5]] + list(refs[5:9])
        if has_s0:
            per_head.append(refs[9].at[h])
        o_ref, sf_ref, st_scr, pad_scr = refs[n_in:]
        _hgrn_head(*per_head, o_ref.at[:, cols], sf_ref.at[h], st_scr.at[h], pad_scr.at[h],
                   tt=tt, blk=blk, has_s0=has_s0)


def _hgrn_head(*refs, tt, blk, has_s0):
    hq_ref, hf_ref, hi_ref, hg_ref, lb_ref, gn_ref, tri_ref, ones2_ref, lvl_ref = refs[:9]
    s0_ref = refs[9] if has_s0 else None
    o_ref, sf_ref, st_scr, pad_scr = refs[-4:]
    t = pl.program_id(2)

    @pl.when(t == 0)
    def _():
        if has_s0:
            st_scr[...] = s0_ref[...].T
        else:
            st_scr[...] = jnp.zeros_like(st_scr)
        pad_scr[:, 0:PAD, :] = jnp.zeros((3, PAD, LANES), F32)

    lb = lb_ref[...]
    sig = jax.nn.sigmoid(hf_ref[...])
    logf = jnp.log(jnp.maximum(lb + (1.0 - lb) * sig, TINY)) * math.log2(math.e)
    kk = (1.0 - lb) * (1.0 - sig)
    qq = _silu(hq_ref[...])
    vv = hi_ref[...]

    tri = tri_ref[...]
    parts = _split3(logf)
    b = jnp.concatenate(
        [sum(jnp.dot(tri, part[c * blk:(c + 1) * blk], preferred_element_type=F32)
             for part in parts) for c in range(tt // blk)], axis=0)

    pad_scr[0, PAD:, :] = kk
    pad_scr[1, PAD:, :] = b
    pad_scr[2, PAD:, :] = vv
    ones2 = ones2_ref[...]
    pos = lax.broadcasted_iota(jnp.int32, (PAD, LANES), 0) % SUB

    def lag_terms(d):
        if d == 0:
            return (qq * kk).astype(BF16), vv
        ks, bs, vs = (pad_scr[r, PAD - d:PAD - d + tt, :] for r in range(3))
        off = jnp.tile(jnp.where(pos >= d, 0.0, MASK_VALUE), (tt // PAD, 1))
        return (qq * ks * jnp.exp2(b - bs + off)).astype(BF16), vs

    o = jnp.zeros((tt, LANES), F32)
    for d in range(0, SUB, 2):
        (p0, v0), (p1, v1) = lag_terms(d), lag_terms(d + 1)
        r = jnp.dot(jnp.concatenate([p0, p1], axis=1), ones2, preferred_element_type=F32)
        o = o + r[:, :LANES] * v0 + r[:, LANES:] * v1

    st = st_scr[...]
    older = []
    for c in range(tt // blk):
        sl = slice(c * blk, (c + 1) * blk)
        bc = b[sl]
        b_last = bc[blk - 1:blk]
        qt = (qq[sl] * jnp.exp2(bc)).astype(BF16)
        older.append(lax.dot_general(qt, st.astype(BF16), _NT, preferred_element_type=F32))
        kt = (kk[sl] * jnp.exp2(b_last - bc)).astype(BF16)
        st = st * jnp.exp2(b_last) + lax.dot_general(vv[sl].astype(BF16), kt, _TN,
                                                    preferred_element_type=F32)
    st_scr[...] = st
    o = o + jnp.concatenate(older, axis=0)

    weights = [None] * (tt // blk)
    m, level = SUB, 0
    while 2 * m <= blk:
        b3 = b.reshape(tt // (2 * m), 2 * m, LANES)
        w = jnp.exp2(-jnp.abs(b3 - b3[:, m - 1:m, :])).reshape(tt, LANES)
        qt = (qq * w).astype(BF16)
        kt = (kk * w).astype(BF16)
        for c in range(tt // blk):
            sl = slice(c * blk, (c + 1) * blk)
            a = lax.dot_general(qt[sl], kt[sl], _NT, preferred_element_type=F32) * lvl_ref[level]
            weights[c] = a if weights[c] is None else weights[c] + a
        m, level = 2 * m, level + 1
    if level:
        o = o + jnp.concatenate(
            [jnp.dot(weights[c].astype(BF16), vv[c * blk:(c + 1) * blk].astype(BF16),
                     preferred_element_type=F32) for c in range(tt // blk)], axis=0)

    y = o * lax.rsqrt(jnp.mean(o * o, axis=-1, keepdims=True) + EPS) * gn_ref[...]
    o_ref[...] = (y * _silu(hg_ref[...])).astype(BF16)

    @pl.when(t == pl.num_programs(2) - 1)
    def _():
        sf_ref[...] = st.T


def _hgrn(z, lb, gn, s0, layer, *, nb, t, nh, tt, blk, heads):
    bt = z.shape[0]
    nt = t // tt
    has_s0 = s0 is not None
    ng = nh // heads
    wide = heads * LANES

    def col(group):
        return pl.BlockSpec((tt, wide), lambda b, h, i: (b * nt + i, group * ng + h))

    state = pl.BlockSpec((None, heads, LANES, LANES), lambda b, h, i: (b, h, 0, 0))
    tri = jnp.tril(jnp.ones((blk, blk), BF16))
    ones2 = jnp.kron(jnp.eye(2, dtype=BF16), jnp.ones((LANES, LANES), BF16))
    pos = jnp.arange(blk)
    sizes = [SUB << i for i in range(max(blk // SUB, 1).bit_length() - 1)]
    lvl = jnp.stack([((pos[:, None] // (2 * m) == pos[None, :] // (2 * m))
                      & (pos[:, None] % (2 * m) >= m) & (pos[None, :] % (2 * m) < m))
                     for m in sizes]).astype(F32) if sizes else jnp.zeros((1, blk, blk), F32)
    in_specs = [col(0), col(1), col(2), col(3),
                pl.BlockSpec((1, wide), lambda b, h, i: (0, h)),
                pl.BlockSpec((1, LANES), lambda b, h, i: (0, 0)),
                _resident(tri.shape), _resident(ones2.shape), _resident(lvl.shape)]
    args = [z, z, z, z, lb, gn, tri, ones2, lvl]
    if has_s0:
        in_specs.append(pl.BlockSpec((None, None, heads, LANES, LANES),
                                     lambda b, h, i: (layer, b, h, 0, 0)))
        args.append(s0)
    return pl.pallas_call(
        functools.partial(_hgrn_kernel, heads=heads, tt=tt, blk=blk, has_s0=has_s0),
        grid=(nb, ng, nt),
        in_specs=in_specs,
        out_specs=[pl.BlockSpec((tt, wide), lambda b, h, i: (b * nt + i, h)), state],
        out_shape=[jax.ShapeDtypeStruct((bt, nh * LANES), BF16),
                   jax.ShapeDtypeStruct((nb, nh, LANES, LANES), F32)],
        scratch_shapes=[pltpu.VMEM((heads, LANES, LANES), F32),
                        pltpu.VMEM((heads, 3, PAD + tt, LANES), F32)],
        compiler_params=_params("parallel", "parallel", "arbitrary"),
        name="hgrn",
    )(*args)


def _merge_kernel(x_ref, gt_ref, oa_ref, oh_ref, ga_ref, gh_ref, wa_ref, wh_ref, wo_ref, o_ref):
    a = jnp.dot(oa_ref[...], wa_ref[...], preferred_element_type=F32)
    r = jnp.dot(oh_ref[...], wh_ref[...], preferred_element_type=F32)
    merged = ga_ref[...].astype(F32) * a + gh_ref[...].astype(F32) * r
    o_ref[...] = x_ref[...] + gt_ref[...] * jnp.dot(merged.astype(BF16), wo_ref[...],
                                                    preferred_element_type=F32)


def _merge(x, gt, oa, oh, gates, wa, wh, wo, *, tm, per_row, rows_per_seq):
    t, d = x.shape
    mod = _mod_spec(per_row, tm, d, rows_per_seq)
    row = pl.BlockSpec((tm, d), lambda i: (i, 0))
    return pl.pallas_call(
        _merge_kernel,
        grid=(t // tm,),
        in_specs=[row, mod, row, row,
                  row, pl.BlockSpec((tm, d), lambda i: (i, 1)),
                  _resident(wa.shape), _resident(wh.shape), _resident(wo.shape)],
        out_specs=row,
        out_shape=jax.ShapeDtypeStruct((t, d), F32),
        compiler_params=_params("parallel"),
        name="merge",
    )(x, gt, oa, oh, gates, gates, wa, wh, wo)


def _tile(n, target):
    if n <= target:
        return n
    for c in range(target, SUBLANES - 1, -1):
        if n % c == 0 and c % SUBLANES == 0:
            return c
    return n


def _layer_weights(p, l, d, fc):
    f = p["w_ffn1_d"].shape[1]
    nc = f // fc

    def ffn(w_gu, w_d):
        wg = w_gu[l, :, :f].reshape(d, nc, fc).transpose(1, 0, 2).astype(BF16)
        wu = w_gu[l, :, f:].reshape(d, nc, fc).transpose(1, 0, 2).astype(BF16)
        return wg, wu, w_d[l].reshape(nc, fc, d).astype(BF16)

    ng = p["w_in"].shape[2] // d
    return dict(
        ffn1=ffn(p["w_ffn1_gu"], p["w_ffn1_d"]),
        ffn2=ffn(p["w_ffn2_gu"], p["w_ffn2_d"]),
        w_in=p["w_in"][l].reshape(d, ng, d).transpose(1, 0, 2).astype(BF16),
        w_ba=p["w_br_att"][l].astype(BF16),
        w_bh=p["w_br_hg"][l].astype(BF16),
        w_out=p["w_out"][l].astype(BF16),
    )


def _trunk(x3, mods, p, lb_all, weights, cache):
    nb, t, d = x3.shape
    depth = p["w_ada"].shape[0]
    nh = d // LANES
    bt = nb * t
    x = x3.reshape(bt, d)
    per_row = t < INPROJ_TILE
    tm = _tile(bt if per_row else t, ROW_TILE)
    kst = jnp.zeros((depth, bt * nh, LANES), F32)
    vst = jnp.zeros((depth, bt * nh, LANES), F32)
    states = []
    for l in range(depth):
        w = weights[l]
        m = mods[l]
        if per_row:
            mm = [jnp.repeat(m[:, i], t, axis=0) for i in range(N_ADA)]
        else:
            mm = [m[:, i].reshape(nb, 1, d) for i in range(N_ADA)]
        sh1, sc1, g1, sh2, sc2, g2, sh3, sc3, g3 = mm
        kw = dict(tm=tm, per_row=per_row, rows_per_seq=t)
        row = lambda a: a.reshape(1, -1)

        g_final = row(p["g_final"])
        kw_ffn = dict(kw, tm=_tile(bt if per_row else t, FFN_TILE))
        x = _ffn(x, row(p["g_ffn1"][l]), sh1, sc1, g1, *w["ffn1"], g_final, final_norm=False,
                 **kw_ffn)
        q_scale = (LANES // 2) ** -0.5 * math.log2(math.e)
        kw_in = dict(kw, tm=_tile(tm, INPROJ_TILE))
        z, qkv, kst, vst, gates = _inproj(x, row(p["g_mix"][l]), sh2, sc2, w["w_in"], kst, vst, l,
                                   q_scale=q_scale, **kw_in)

        lam_init = 0.8 - 0.6 * math.exp(-0.3 * l)
        cst = jnp.array([lam_init], F32)
        gsub = row(p["g_att_sub"][l])
        if cache is None:
            oa = _attn(cst, p["att_lambda"][l], gsub, qkv, nb=nb, t=t, nh=nh,
                       tq=_tile(t, Q_TILE))
            s0 = None
        else:
            ck, cv, s0 = cache
            oa = _attn_cached(cst, p["att_lambda"][l], gsub, qkv, ck, cv, l, nb=nb, t=t, nh=nh)
        tt = _tile(t, HGRN_TILE)
        blk = min(tt, 128)
        oh, s_fin = _hgrn(z, row(lb_all[l]), row(p["g_hg_norm"][l]), s0, l,
                          nb=nb, t=t, nh=nh, tt=tt, blk=blk, heads=nh if per_row else 1)
        states.append(s_fin)

        x = _merge(x, g2, oa, oh, gates, w["w_ba"], w["w_bh"], w["w_out"], **kw)
        x = _ffn(x, row(p["g_ffn2"][l]), sh3, sc3, g3, *w["ffn2"], g_final,
                 final_norm=l == depth - 1, **kw_ffn)

    y = x.reshape(nb, t, d)
    shape5 = (depth, nb, t, nh, LANES)
    return y, kst.reshape(shape5), vst.reshape(shape5), jnp.stack(states)


def kernel(x_prompt, x_sample, cache_k, cache_v, state_hgrn, c_prompt, c_sample, w_ada, b_ada, g_ffn1, w_ffn1_gu, w_ffn1_d, g_mix, w_in, att_lambda, g_att_sub, hg_lb_logits, g_hg_norm, w_br_att, w_br_hg, w_out, g_ffn2, w_ffn2_gu, w_ffn2_d, g_final):
    p = dict(w_ada=w_ada, b_ada=b_ada, g_ffn1=g_ffn1, w_ffn1_gu=w_ffn1_gu, w_ffn1_d=w_ffn1_d,
             g_mix=g_mix, w_in=w_in, att_lambda=att_lambda, g_att_sub=g_att_sub,
             g_hg_norm=g_hg_norm, w_br_att=w_br_att, w_br_hg=w_br_hg, w_out=w_out,
             g_ffn2=g_ffn2, w_ffn2_gu=w_ffn2_gu, w_ffn2_d=w_ffn2_d, g_final=g_final)
    depth, d, _ = w_ada.shape
    nbp, nbs = x_prompt.shape[0], x_sample.shape[0]

    lb_sm = jax.nn.softmax(hg_lb_logits.astype(F32), axis=0)
    lb_all = jnp.cumsum(lb_sm, axis=0) - lb_sm[0]

    ns = nbp + nbs
    nsp = -(-ns // SUBLANES) * SUBLANES
    c_all = jnp.concatenate([c_prompt, c_sample, jnp.zeros((nsp - ns, d), F32)], axis=0)
    mods = _ada(c_all, w_ada, b_ada).reshape(depth, nsp, N_ADA, d)

    f = w_ffn1_d.shape[1]
    fc = MXU_DIM if f % MXU_DIM == 0 else LANES
    weights = [_layer_weights(p, l, d, fc) for l in range(depth)]

    y_p, k_p, v_p, s_p = _trunk(x_prompt, mods[:, :nbp], p, lb_all, weights, None)
    y_s, k_s, v_s, s_s = _trunk(x_sample, mods[:, nbp:ns], p, lb_all, weights,
                                (cache_k, cache_v, state_hgrn))
    return (y_p, y_s, k_p, v_p, s_p, k_s, v_s, s_s)
```
